```python
import math
import jax, jax.numpy as jnp
from jax import lax
import numpy as np

D_MODEL = 1024
BATCH = 4
SEQ = 8192
DEPTH = 2

MEM_LEN = 256
HEAD_DIM = 64
NSA_HEADS = 8
NSA_KV_HEADS = 2
NSA_HPG = NSA_HEADS // NSA_KV_HEADS
NSA_WIDTH = NSA_HEADS * HEAD_DIM
KV_WIDTH = NSA_KV_HEADS * HEAD_DIM
CMP_BLOCK = 64
SEL_TOPN = 16
WINDOW = 512
Q_BLOCK = 128
FORCE_SCORE = 1e4
LRU_WIDTH = D_MODEL // 2
LRU_BLOCKS = 8
LRU_BLOCK_DIM = LRU_WIDTH // LRU_BLOCKS
CONV_WIDTH = 4
LRU_C = 8.0
MIX_WIDTH = NSA_WIDTH + LRU_WIDTH
IN_COLS = NSA_WIDTH + 6 * KV_WIDTH + 3 * NSA_HEADS + 2 * LRU_WIDTH
X_HEADS = 4
X_HEAD_DIM = D_MODEL // X_HEADS
D_FF = 4 * D_MODEL
N_BUCKETS = 32
MAX_DISTANCE = 128
EPS = 1e-6
NEG = -1e30

kernel_name = "hymba_nsa_rglru_sandwich_trunk"


def rmsnorm(x, g):
    xf = x.astype(jnp.float32)
    r = xf * lax.rsqrt(jnp.mean(xf * xf, axis=-1, keepdims=True) + EPS)
    return (r * g.astype(jnp.float32)).astype(x.dtype)


def masked_softmax(logits, mask):
    l = jnp.where(mask, logits.astype(jnp.float32), NEG)
    m = jnp.max(l, axis=-1, keepdims=True)
    e = jnp.where(mask, jnp.exp(l - m), 0.0)
    return e / jnp.maximum(jnp.sum(e, axis=-1, keepdims=True), 1e-30)


def t5_bucket(dist):
    max_exact = N_BUCKETS // 2
    d = jnp.maximum(dist, 0)
    df = jnp.maximum(d, 1).astype(jnp.float32)
    large = max_exact + (jnp.log(df / max_exact) / math.log(MAX_DISTANCE / max_exact)
                         * (N_BUCKETS - max_exact)).astype(jnp.int32)
    large = jnp.minimum(large, N_BUCKETS - 1)
    return jnp.where(d < max_exact, d, large)


def nsa_attention(q, kc_raw, vc_raw, ks, vs, kw, vw, gates, pe_k, pe_v, w_ck, w_cv, rel_bias):
    B, S = q.shape[0], q.shape[1]
    G, HPG, Dh = NSA_KV_HEADS, NSA_HPG, HEAD_DIM
    nb = S // CMP_BLOCK
    n_sel = min(SEL_TOPN, nb)
    scale = Dh ** -0.5

    def compress(raw, pe, w):
        blocks = raw.reshape(B, nb, CMP_BLOCK, G, Dh) + pe[None, None, :, None, :]
        return jnp.einsum('bnrgd,rde->bnge', blocks, w)

    kc = compress(kc_raw, pe_k, w_ck)
    vc = compress(vc_raw, pe_v, w_cv)
    ks_blk = ks.reshape(B, nb, CMP_BLOCK, G, Dh).transpose(0, 3, 1, 2, 4)
    vs_blk = vs.reshape(B, nb, CMP_BLOCK, G, Dh).transpose(0, 3, 1, 2, 4)
    kw_pad = jnp.pad(kw, ((0, 0), (WINDOW, 0), (0, 0), (0, 0)))
    vw_pad = jnp.pad(vw, ((0, 0), (WINDOW, 0), (0, 0), (0, 0)))
    bias_g = rel_bias.reshape(N_BUCKETS, G, HPG)
    blk_idx = jnp.arange(nb)
    blk_end = blk_idx * CMP_BLOCK + CMP_BLOCK - 1
    b_ix = jnp.arange(B)[:, None, None, None]
    g_ix = jnp.arange(G)[None, :, None, None]

    def one_block(s0):
        t = s0 + jnp.arange(Q_BLOCK)
        qb = lax.dynamic_slice_in_dim(q, s0, Q_BLOCK, axis=1)
        gb = lax.dynamic_slice_in_dim(gates, s0, Q_BLOCK, axis=1)

        dist_c = t[:, None] - blk_end[None, :]
        mask_c = dist_c >= 0
        bias_c = bias_g[t5_bucket(dist_c)].transpose(2, 3, 0, 1)
        s_c = jnp.einsum('bqghd,bngd->bghqn', qb, kc) * scale + bias_c
        p_c = masked_softmax(s_c, mask_c)
        o_c = jnp.einsum('bghqn,bngd->bqghd', p_c.astype(vc.dtype), vc)

        importance = jnp.sum(p_c, axis=2)
        cur = (t // CMP_BLOCK)[:, None]
        j = blk_idx[None, :]
        forced = (j == 0) | (j == cur) | (j == cur - 1)
        sel_score = jnp.where(j <= cur, jnp.where(forced, FORCE_SCORE, importance), NEG)
        top_val, top_idx = lax.top_k(sel_score, n_sel)
        blk_valid = top_val > NEG / 2
        k_sel = ks_blk[b_ix, g_ix, top_idx]
        v_sel = vs_blk[b_ix, g_ix, top_idx]
        pos = top_idx[..., None] * CMP_BLOCK + jnp.arange(CMP_BLOCK)
        dist_s = t[None, None, :, None, None] - pos
        mask_s = blk_valid[..., None] & (dist_s >= 0)
        bias_s = jnp.moveaxis(bias_g[t5_bucket(dist_s), g_ix[..., None]], -1, 2)
        s_s = jnp.einsum('bqghd,bgqnrd->bghqnr', qb, k_sel) * scale + bias_s
        flat = (B, G, HPG, Q_BLOCK, n_sel * CMP_BLOCK)
        mask_s_full = jnp.broadcast_to(mask_s[:, :, None], s_s.shape).reshape(flat)
        p_s = masked_softmax(s_s.reshape(flat), mask_s_full).reshape(s_s.shape)
        o_s = jnp.einsum('bghqnr,bgqnrd->bqghd', p_s.astype(v_sel.dtype), v_sel)

        kwb = lax.dynamic_slice_in_dim(kw_pad, s0, WINDOW + Q_BLOCK, axis=1)
        vwb = lax.dynamic_slice_in_dim(vw_pad, s0, WINDOW + Q_BLOCK, axis=1)
        kpos = s0 - WINDOW + jnp.arange(WINDOW + Q_BLOCK)
        dist_w = t[:, None] - kpos[None, :]
        mask_w = (kpos[None, :] >= 0) & (dist_w >= 0) & (dist_w < WINDOW)
        bias_w = bias_g[t5_bucket(dist_w)].transpose(2, 3, 0, 1)
        s_w = jnp.einsum('bqghd,bkgd->bghqk', qb, kwb) * scale + bias_w
        p_w = masked_softmax(s_w, mask_w)
        o_w = jnp.einsum('bghqk,bkgd->bqghd', p_w.astype(vwb.dtype), vwb)

        g = jax.nn.sigmoid(gb.astype(jnp.float32))
        out = g[..., 0:1] * o_c + g[..., 1:2] * o_s + g[..., 2:3] * o_w
        return out.reshape(B, Q_BLOCK, NSA_WIDTH).astype(q.dtype)

    starts = jnp.arange(S // Q_BLOCK) * Q_BLOCK
    outs = lax.map(one_block, starts)
    return outs.transpose(1, 0, 2, 3).reshape(B, S, NSA_WIDTH)


def rg_lru_branch(xg, xr, conv_w, conv_b, w_a, b_a, w_x, b_x, lam):
    B, S, _ = xr.shape
    y = jax.nn.gelu(xg)
    xp = jnp.pad(xr, ((0, 0), (CONV_WIDTH - 1, 0), (0, 0)))
    xc = conv_b + sum(xp[:, k:k + S] * conv_w[k] for k in range(CONV_WIDTH))
    xb = xc.reshape(B, S, LRU_BLOCKS, LRU_BLOCK_DIM)
    r = jax.nn.sigmoid(jnp.einsum('bsnd,nde->bsne', xb, w_a).reshape(B, S, LRU_WIDTH) + b_a)
    i = jax.nn.sigmoid(jnp.einsum('bsnd,nde->bsne', xb, w_x).reshape(B, S, LRU_WIDTH) + b_x)
    log_a = -LRU_C * jax.nn.softplus(-lam.astype(jnp.float32)) * r.astype(jnp.float32)
    a = jnp.exp(log_a)
    b = jnp.sqrt(-jnp.expm1(2.0 * log_a)) * (i * xc).astype(jnp.float32)

    def combine(left, right):
        a1, b1 = left
        a2, b2 = right
        return a1 * a2, a2 * b1 + b2

    _, h = lax.associative_scan(combine, (a, b), axis=1)
    return h.astype(xr.dtype) * y


def setup_inputs(seed: int = 0) -> dict:
    key = jax.random.key(seed)
    ks = jax.random.split(key, 32)
    f32 = jnp.float32

    def nrm(k, shape, scale):
        return jax.random.normal(k, shape, f32) * scale

    def gain(k, shape):
        return 1.0 + 0.05 * jax.random.normal(k, shape, f32)

    L, D = DEPTH, D_MODEL
    a8 = jax.random.uniform(ks[20], (L, LRU_WIDTH), f32, minval=0.9, maxval=0.999)
    a = a8 ** (1.0 / LRU_C)
    lam = jnp.log(a) - jnp.log1p(-a)
    return {
        "x": nrm(ks[0], (BATCH, SEQ, D), 1.0),
        "mem": nrm(ks[1], (BATCH, MEM_LEN, D), 1.0),
        "rel_bias": nrm(ks[2], (N_BUCKETS, NSA_HEADS), 0.5),
        "ln_mix_pre": gain(ks[3], (L, D)),
        "ln_mix_post": gain(ks[4], (L, D)),
        "w_in": nrm(ks[5], (L, D, IN_COLS), D ** -0.5),
        "cmp_pe_k": nrm(ks[6], (L, CMP_BLOCK, HEAD_DIM), 0.1),
        "cmp_pe_v": nrm(ks[7], (L, CMP_BLOCK, HEAD_DIM), 0.1),
        "cmp_w_k": nrm(ks[8], (L, CMP_BLOCK, HEAD_DIM, HEAD_DIM), (CMP_BLOCK * HEAD_DIM) ** -0.5),
        "cmp_w_v": nrm(ks[9], (L, CMP_BLOCK, HEAD_DIM, HEAD_DIM), (CMP_BLOCK * HEAD_DIM) ** -0.5),
        "conv_w": nrm(ks[10], (L, CONV_WIDTH, LRU_WIDTH), CONV_WIDTH ** -0.5),
        "conv_b": nrm(ks[11], (L, LRU_WIDTH), 0.02),
        "lru_wa": nrm(ks[12], (L, LRU_BLOCKS, LRU_BLOCK_DIM, LRU_BLOCK_DIM), LRU_BLOCK_DIM ** -0.5),
        "lru_ba": nrm(ks[13], (L, LRU_WIDTH), 0.02),
        "lru_wx": nrm(ks[14], (L, LRU_BLOCKS, LRU_BLOCK_DIM, LRU_BLOCK_DIM), LRU_BLOCK_DIM ** -0.5),
        "lru_bx": nrm(ks[15], (L, LRU_WIDTH), 0.02),
        "lru_lambda": lam,
        "gn_attn": gain(ks[16], (L, NSA_WIDTH)),
        "gn_lru": gain(ks[17], (L, LRU_WIDTH)),
        "w_out": nrm(ks[18], (L, MIX_WIDTH, D), MIX_WIDTH ** -0.5),
        "ln_x_pre": gain(ks[19], (L, D)),
        "ln_x_post": gain(ks[21], (L, D)),
        "ln_mem": gain(ks[22], (L, D)),
        "xq": nrm(ks[23], (L, D, D), D ** -0.5),
        "xkv": nrm(ks[24], (L, D, 2 * D), D ** -0.5),
        "xo": nrm(ks[25], (L, D, D), D ** -0.5),
        "ln_mlp_pre": gain(ks[26], (L, D)),
        "ln_mlp_post": gain(ks[27], (L, D)),
        "mlp_w1": nrm(ks[28], (L, D, D_FF), D ** -0.5),
        "mlp_w2": nrm(ks[29], (L, D_FF, D), D_FF ** -0.5),
    }


def reference(x, mem, rel_bias, ln_mix_pre, ln_mix_post, w_in, cmp_pe_k, cmp_pe_v, cmp_w_k, cmp_w_v,
              conv_w, conv_b, lru_wa, lru_ba, lru_wx, lru_bx, lru_lambda, gn_attn, gn_lru, w_out,
              ln_x_pre, ln_x_post, ln_mem, xq, xkv, xo, ln_mlp_pre, ln_mlp_post, mlp_w1, mlp_w2):
    B, S, D = x.shape
    M = mem.shape[1]
    sizes = [NSA_WIDTH] + [KV_WIDTH] * 6 + [3 * NSA_HEADS, LRU_WIDTH, LRU_WIDTH]
    offsets = [int(o) for o in np.cumsum(sizes)[:-1]]
    for l in range(DEPTH):
        h = rmsnorm(x, ln_mix_pre[l])
        z = h @ w_in[l]
        q, kc, vc, ksl, vsl, kwn, vwn, gt, xg, xr = jnp.split(z, offsets, axis=-1)
        kvshape = (B, S, NSA_KV_HEADS, HEAD_DIM)
        att = nsa_attention(
            q.reshape(B, S, NSA_KV_HEADS, NSA_HPG, HEAD_DIM),
            kc.reshape(kvshape), vc.reshape(kvshape), ksl.reshape(kvshape), vsl.reshape(kvshape),
            kwn.reshape(kvshape), vwn.reshape(kvshape),
            gt.reshape(B, S, NSA_KV_HEADS, NSA_HPG, 3),
            cmp_pe_k[l], cmp_pe_v[l], cmp_w_k[l], cmp_w_v[l], rel_bias)
        lru = rg_lru_branch(xg, xr, conv_w[l], conv_b[l], lru_wa[l], lru_ba[l],
                            lru_wx[l], lru_bx[l], lru_lambda[l])
        mixed = jnp.concatenate([rmsnorm(att, gn_attn[l]), rmsnorm(lru, gn_lru[l])], axis=-1) @ w_out[l]
        x = x + rmsnorm(mixed, ln_mix_post[l])

        h = rmsnorm(x, ln_x_pre[l])
        mn = rmsnorm(mem, ln_mem[l])
        cq = (h @ xq[l]).reshape(B, S, X_HEADS, X_HEAD_DIM)
        ck, cv = jnp.split((mn @ xkv[l]).reshape(B, M, 2, X_HEADS, X_HEAD_DIM), 2, axis=2)
        ck, cv = ck[:, :, 0], cv[:, :, 0]
        s = jnp.einsum('bshd,bmhd->bhsm', cq, ck).astype(jnp.float32) * (X_HEAD_DIM ** -0.5)
        p = jax.nn.softmax(s, axis=-1).astype(cv.dtype)
        co = jnp.einsum('bhsm,bmhd->bshd', p, cv).reshape(B, S, D) @ xo[l]
        x = x + rmsnorm(co, ln_x_post[l])

        h = rmsnorm(x, ln_mlp_pre[l])
        u = jnp.square(jax.nn.relu(h @ mlp_w1[l]))
        x = x + rmsnorm(u @ mlp_w2[l], ln_mlp_post[l])
    return x
```

```python
import functools
import math

import numpy as np
import jax
import jax.numpy as jnp
from jax import lax
from jax.experimental import pallas as pl
from jax.experimental.pallas import tpu as pltpu

HEAD_DIM = 64
NSA_HEADS = 8
NSA_KV_HEADS = 2
NSA_HPG = NSA_HEADS // NSA_KV_HEADS
NSA_WIDTH = NSA_HEADS * HEAD_DIM
KV_WIDTH = NSA_KV_HEADS * HEAD_DIM
CMP_BLOCK = 64
SEL_TOPN = 16
WINDOW = 512
FORCE_SCORE = 1e4
LRU_BLOCKS = 8
CONV_WIDTH = 4
LRU_C = 8.0
X_HEADS = 4
N_BUCKETS = 32
MAX_DISTANCE = 128
EPS = 1e-6
NEG = -1e30

TQ = 256
CK = 256
MASKV = -(2.0 ** 100)
M_INIT = -1e30
ROW_TILE = 512
LRU_TILE = 512
VMEM_LIMIT = 56 * 1024 * 1024

F32 = jnp.float32
BF16 = jnp.bfloat16


def _bucket_of_distance(d):
    max_exact = N_BUCKETS // 2
    d = np.maximum(d, 0)
    df = np.maximum(d, 1).astype(np.float64)
    large = max_exact + (np.log(df / max_exact) / math.log(MAX_DISTANCE / max_exact)
                         * (N_BUCKETS - max_exact)).astype(np.int32)
    large = np.minimum(large, N_BUCKETS - 1)
    return np.where(d < max_exact, d, large).astype(np.int32)


def _rms(x, g):
    return x * lax.rsqrt(jnp.mean(x * x, axis=-1, keepdims=True) + EPS) * g


def _dot(a, b):
    return jnp.dot(a, b, preferred_element_type=F32)


def _params(sem):
    return pltpu.CompilerParams(dimension_semantics=sem, vmem_limit_bytes=VMEM_LIMIT)


def _const_spec(shape):
    nd = len(shape)
    return pl.BlockSpec(shape, lambda *_: (0,) * nd)


def _inproj_kernel(x_ref, g_ref, wq_ref, wc_ref, wkv_ref, wg_ref, wx_ref,
                   q_ref, kcr_ref, vcr_ref, kv_ref, gt_ref, xgr_ref):
    h = _rms(x_ref[...], g_ref[...]).astype(BF16)
    q_ref[...] = (_dot(h, wq_ref[...]) * (HEAD_DIM ** -0.5)).astype(BF16)
    c = _dot(h, wc_ref[...])
    kcr_ref[...] = c[:, :KV_WIDTH]
    vcr_ref[...] = c[:, KV_WIDTH:]
    kv_ref[...] = _dot(h, wkv_ref[...]).astype(BF16)
    gt_ref[...] = _dot(h, wg_ref[...])
    xgr_ref[...] = _dot(h, wx_ref[...])


def _inproj(x2, gain, wq, wc, wkv, wg, wx):
    n, d = x2.shape
    tm = ROW_TILE
    row = lambda w: pl.BlockSpec((tm, w), lambda i: (i, 0))
    return pl.pallas_call(
        _inproj_kernel,
        grid=(n // tm,),
        in_specs=[row(d), _const_spec(gain.shape), _const_spec(wq.shape), _const_spec(wc.shape),
                  _const_spec(wkv.shape), _const_spec(wg.shape), _const_spec(wx.shape)],
        out_specs=[row(NSA_WIDTH), row(KV_WIDTH), row(KV_WIDTH), row(4 * KV_WIDTH), row(128),
                   row(wx.shape[1])],
        out_shape=[jax.ShapeDtypeStruct((n, NSA_WIDTH), BF16),
                   jax.ShapeDtypeStruct((n, KV_WIDTH), F32),
                   jax.ShapeDtypeStruct((n, KV_WIDTH), F32),
                   jax.ShapeDtypeStruct((n, 4 * KV_WIDTH), BF16),
                   jax.ShapeDtypeStruct((n, 128), F32),
                   jax.ShapeDtypeStruct((n, wx.shape[1]), F32)],
        compiler_params=_params(("parallel",)),
        name="inproj",
    )(x2, gain, wq, wc, wkv, wg, wx)


def _compress_kernel(kr_ref, vr_ref, pek_ref, pev_ref, wk_ref, wv_ref, kc_ref, vc_ref):
    kc_ref[...] = _dot((kr_ref[...] + pek_ref[...]).astype(BF16), wk_ref[...]).astype(BF16)
    vc_ref[...] = _dot((vr_ref[...] + pev_ref[...]).astype(BF16), wv_ref[...]).astype(BF16)


def _compress(kr, vr, pek, pev, wk, wv):
    nblk, width = kr.shape
    tm = 128
    row = lambda w: pl.BlockSpec((tm, w), lambda i: (i, 0))
    return pl.pallas_call(
        _compress_kernel,
        grid=(nblk // tm,),
        in_specs=[row(width), row(width), _const_spec(pek.shape), _const_spec(pev.shape),
                  _const_spec(wk.shape), _const_spec(wv.shape)],
        out_specs=[row(KV_WIDTH), row(KV_WIDTH)],
        out_shape=[jax.ShapeDtypeStruct((nblk, KV_WIDTH), BF16)] * 2,
        compiler_params=_params(("parallel",)),
        name="compress",
    )(kr, vr, pek, pev, wk, wv)


def _bias_tables_kernel(rb_ref, bma_ref, bmb_ref, bm1_ref, bm2_ref,
                        ta_ref, tb_ref, d1_ref, d2_ref):
    last = N_BUCKETS - 1
    for hd in range(NSA_HEADS):
        g, hl = divmod(hd, NSA_HPG)
        rows = slice(hl * TQ, (hl + 1) * TQ)
        far = rb_ref[last, hd]
        for bm_ref, out_ref, masked in ((bma_ref, ta_ref, False), (bmb_ref, tb_ref, True),
                                        (bm1_ref, d1_ref, False), (bm2_ref, d2_ref, False)):
            bm = bm_ref[...]
            acc = jnp.zeros(bm.shape, F32)
            for k in range(N_BUCKETS - 1):
                acc = jnp.where(bm == k, rb_ref[k, hd] - far, acc)
            if masked:
                acc = jnp.where(bm < 0, MASKV, acc)
            out_ref[g, rows, :] = acc


def _bias_tables(rel_bias):
    qi = np.arange(TQ)[:, None]
    ki = np.arange(CK)[None, :]
    bma = _bucket_of_distance(qi - ki + CK)
    bmb = np.where(qi >= ki, _bucket_of_distance(qi - ki), -1).astype(np.int32)
    r = (np.arange(TQ) + 1) % CMP_BLOCK
    bm1 = np.broadcast_to(_bucket_of_distance(r)[:, None], (TQ, 128)).astype(np.int32)
    bm2 = np.broadcast_to(_bucket_of_distance(r + CMP_BLOCK)[:, None], (TQ, 128)).astype(np.int32)
    rows = NSA_HPG * TQ
    return pl.pallas_call(
        _bias_tables_kernel,
        in_specs=[pl.BlockSpec(memory_space=pltpu.SMEM)] + [pl.BlockSpec(memory_space=pltpu.VMEM)] * 4,
        out_specs=[pl.BlockSpec(memory_space=pltpu.VMEM)] * 4,
        out_shape=[jax.ShapeDtypeStruct((NSA_KV_HEADS, rows, CK), F32),
                   jax.ShapeDtypeStruct((NSA_KV_HEADS, rows, CK), F32),
                   jax.ShapeDtypeStruct((NSA_KV_HEADS, rows, 128), F32),
                   jax.ShapeDtypeStruct((NSA_KV_HEADS, rows, 128), F32)],
        name="bias_tables",
    )(rel_bias, jnp.asarray(bma), jnp.asarray(bmb), jnp.asarray(bm1), jnp.asarray(bm2))


def _flash_update(s, v, m_ref, l_ref, acc_ref):
    m_old = m_ref[...]
    m_new = jnp.maximum(m_old, jnp.max(s, axis=1, keepdims=True))
    alpha = jnp.exp(m_old - m_new)
    p = jnp.exp(s - jnp.concatenate([m_new] * (s.shape[1] // 128), axis=1))
    l_ref[...] = alpha * l_ref[...] + jnp.sum(p, axis=1, keepdims=True)
    acc_ref[...] = alpha * acc_ref[...] + _dot(p.astype(BF16), v)
    m_ref[...] = m_new


def _attn_kernel(q_ref, gt_ref, kc_ref, vc_ref, ks_ref, vs_ref, kw_ref, vw_ref,
                 ta_ref, tb_ref, d1_ref, d2_ref, out_ref,
                 kat_ref, kwt_ref, kct_ref, qa_ref, m_ref, l_ref, acc_ref, os_ref, oc_ref):
    qt = pl.program_id(1)
    rows = NSA_HPG * TQ
    n_blocks = kc_ref.shape[1]

    @pl.when(qt == 0)
    def _():
        def build(c, carry):
            rs = pl.ds(pl.multiple_of(c * CK, CK), CK)
            kst = ks_ref[0, rs, :].astype(F32).T
            kwt = kw_ref[0, rs, :].astype(F32).T
            blk = lax.broadcasted_iota(jnp.int32, (128, CK), 0)
            key_blk = c * (CK // CMP_BLOCK) + (lax.broadcasted_iota(jnp.int32, (128, CK), 1) >> 6)
            onehot = jnp.where(blk == key_blk, 1.0, 0.0).astype(BF16)
            for g in range(NSA_KV_HEADS):
                gs = slice(g * HEAD_DIM, (g + 1) * HEAD_DIM)
                kat_ref[g, c, 0:128, :] = onehot
                kat_ref[g, c, 128:128 + HEAD_DIM, :] = kst[gs].astype(BF16)
                kat_ref[g, c, 128 + HEAD_DIM:, :] = jnp.zeros((CK - 128 - HEAD_DIM, CK), BF16)
                kwt_ref[g, c] = kwt[gs].astype(BF16)
            return carry
        lax.fori_loop(0, ks_ref.shape[1] // CK, build, 0)
        kct_ref[...] = kc_ref[0].astype(F32).T.astype(BF16)
        qa_ref[:, 128 + HEAD_DIM:] = jnp.zeros((rows, CK - 128 - HEAD_DIM), BF16)

    q_tile = q_ref[0].astype(F32)
    gates = jax.nn.sigmoid(gt_ref[0])
    vc = vc_ref[0]

    row_id = lax.broadcasted_iota(jnp.int32, (rows, 128), 0)
    q_loc = row_id & (TQ - 1)
    n_iota = lax.broadcasted_iota(jnp.int32, (rows, 128), 1)
    n1 = qt * (TQ // CMP_BLOCK) + ((q_loc + 1) >> 6) - 1
    ql_s = lax.broadcasted_iota(jnp.int32, (TQ, 128), 0)
    j_s = lax.broadcasted_iota(jnp.int32, (TQ, 128), 1)
    cur = qt * (TQ // CMP_BLOCK) + (ql_s >> 6)
    qi_w = lax.broadcasted_iota(jnp.int32, (rows, CK), 0) & (TQ - 1)
    ki_w = lax.broadcasted_iota(jnp.int32, (rows, CK), 1)
    w2_mask = jnp.where(qi_w < ki_w, 0.0, MASKV)

    for g in range(NSA_KV_HEADS):
        gs = slice(g * HEAD_DIM, (g + 1) * HEAD_DIM)
        q_rows = jnp.concatenate(
            [q_tile[:, (g * NSA_HPG + hl) * HEAD_DIM:(g * NSA_HPG + hl + 1) * HEAD_DIM]
             for hl in range(NSA_HPG)], axis=0).astype(BF16)

        s_c = _dot(q_rows, kct_ref[gs, :])
        bias_c = (jnp.where(n_iota == n1, d1_ref[g], 0.0)
                  + jnp.where(n_iota == n1 - 1, d2_ref[g], 0.0))
        valid_c = n_iota <= n1
        l_c = jnp.where(valid_c, s_c + bias_c, NEG)
        m_c = jnp.max(l_c, axis=1, keepdims=True)
        e_c = jnp.where(valid_c, jnp.exp(l_c - m_c), 0.0)
        p_c = e_c / jnp.maximum(jnp.sum(e_c, axis=1, keepdims=True), 1e-30)
        oc_ref[...] = _dot(p_c.astype(BF16), vc)

        imp = p_c[0:TQ]
        for hl in range(1, NSA_HPG):
            imp = imp + p_c[hl * TQ:(hl + 1) * TQ]
        forced = (j_s == 0) | (j_s == cur) | (j_s == cur - 1)
        work = jnp.where(j_s <= cur, jnp.where(forced, FORCE_SCORE, imp), NEG)
        chosen = jnp.zeros((TQ, n_blocks), F32)
        j_f = j_s.astype(F32)
        for _ in range(min(SEL_TOPN, n_blocks)):
            mx = jnp.max(work, axis=1, keepdims=True)
            first = jnp.min(jnp.where(work == mx, j_f, float(n_blocks)), axis=1, keepdims=True)
            pick = j_f == first
            chosen = jnp.where(pick & (mx > NEG / 2), 1.0, chosen)
            work = jnp.where(pick, -jnp.inf, work)
        sel_mask = jnp.where(chosen > 0.5, 0.0, MASKV).astype(BF16)
        for hl in range(NSA_HPG):
            qa_ref[hl * TQ:(hl + 1) * TQ, 0:128] = sel_mask
        qa_ref[:, 128:128 + HEAD_DIM] = q_rows

        m_ref[...] = jnp.full((rows, 128), M_INIT, F32)
        l_ref[...] = jnp.zeros((rows, 128), F32)
        acc_ref[...] = jnp.zeros((rows, 128), F32)

        def far_chunk(c, carry):
            s = _dot(qa_ref[...], kat_ref[g, c])
            v = vs_ref[0, pl.ds(pl.multiple_of(c * CK, CK), CK), :]
            _flash_update(s, v, m_ref, l_ref, acc_ref)
            return carry
        lax.fori_loop(0, jnp.maximum(qt - 1, 0), far_chunk, 0)

        @pl.when(qt >= 1)
        def _():
            s = _dot(qa_ref[...], kat_ref[g, qt - 1]) + ta_ref[g]
            v = vs_ref[0, pl.ds(pl.multiple_of((qt - 1) * CK, CK), CK), :]
            _flash_update(s, v, m_ref, l_ref, acc_ref)

        s = _dot(qa_ref[...], kat_ref[g, qt]) + tb_ref[g]
        v = vs_ref[0, pl.ds(pl.multiple_of(qt * CK, CK), CK), :]
        _flash_update(s, v, m_ref, l_ref, acc_ref)
        os_ref[...] = acc_ref[...] / l_ref[...]

        m_ref[...] = jnp.full((rows, 128), M_INIT, F32)
        l_ref[...] = jnp.zeros((rows, 128), F32)
        acc_ref[...] = jnp.zeros((rows, 128), F32)

        @pl.when(qt >= 2)
        def _():
            s = _dot(q_rows, kwt_ref[g, qt - 2]) + w2_mask
            v = vw_ref[0, pl.ds(pl.multiple_of((qt - 2) * CK, CK), CK), :]
            _flash_update(s, v, m_ref, l_ref, acc_ref)

        @pl.when(qt >= 1)
        def _():
            s = _dot(q_rows, kwt_ref[g, qt - 1]) + ta_ref[g]
            v = vw_ref[0, pl.ds(pl.multiple_of((qt - 1) * CK, CK), CK), :]
            _flash_update(s, v, m_ref, l_ref, acc_ref)

        s = _dot(q_rows, kwt_ref[g, qt]) + tb_ref[g]
        v = vw_ref[0, pl.ds(pl.multiple_of(qt * CK, CK), CK), :]
        _flash_update(s, v, m_ref, l_ref, acc_ref)
        o_w = acc_ref[...] / l_ref[...]

        o_c = oc_ref[...]
        o_s = os_ref[...]
        for hl in range(NSA_HPG):
            hd = g * NSA_HPG + hl
            rs = slice(hl * TQ, (hl + 1) * TQ)
            mixed = (gates[:, 3 * hd:3 * hd + 1] * o_c[rs]
                     + gates[:, 3 * hd + 1:3 * hd + 2] * o_s[rs]
                     + gates[:, 3 * hd + 2:3 * hd + 3] * o_w[rs])
            out_ref[0, :, hd * HEAD_DIM:(hd + 1) * HEAD_DIM] = mixed[:, gs]


def _attention(q, gates, kc, vc, kv, tabs):
    b, s, _ = q.shape
    n_blocks = s // CMP_BLOCK
    nch = s // CK
    rows = NSA_HPG * TQ
    ta, tb, d1, d2 = tabs
    tile = lambda w: pl.BlockSpec((1, TQ, w), lambda bi, qi: (bi, qi, 0))
    per_b = lambda r, w, col: pl.BlockSpec((1, r, w), lambda bi, qi: (bi, 0, col))
    return pl.pallas_call(
        _attn_kernel,
        grid=(b, s // TQ),
        in_specs=[tile(NSA_WIDTH), tile(128),
                  per_b(n_blocks, KV_WIDTH, 0), per_b(n_blocks, KV_WIDTH, 0),
                  per_b(s, KV_WIDTH, 0), per_b(s, KV_WIDTH, 1), per_b(s, KV_WIDTH, 2), per_b(s, KV_WIDTH, 3),
                  _const_spec(ta.shape), _const_spec(tb.shape), _const_spec(d1.shape), _const_spec(d2.shape)],
        out_specs=tile(NSA_WIDTH),
        out_shape=jax.ShapeDtypeStruct((b, s, NSA_WIDTH), F32),
        scratch_shapes=[pltpu.VMEM((NSA_KV_HEADS, nch, CK, CK), BF16),
                        pltpu.VMEM((NSA_KV_HEADS, nch, HEAD_DIM, CK), BF16),
                        pltpu.VMEM((KV_WIDTH, n_blocks), BF16),
                        pltpu.VMEM((rows, CK), BF16),
                        pltpu.VMEM((rows, 128), F32), pltpu.VMEM((rows, 128), F32),
                        pltpu.VMEM((rows, 128), F32), pltpu.VMEM((rows, 128), F32),
                        pltpu.VMEM((rows, 128), F32)],
        compiler_params=_params(("parallel", "arbitrary")),
        name="nsa_attention",
    )(q, gates, kc, vc, kv, kv, kv, kv, ta, tb, d1, d2)


def _lru_kernel(xgr_ref, cw_ref, cb_ref, wa_ref, ba_ref, wx_ref, bx_ref, lam_ref,
                out_ref, xbuf_ref, h_ref):
    t = pl.program_id(1)
    ts = LRU_TILE
    w = out_ref.shape[2]

    @pl.when(t == 0)
    def _():
        xbuf_ref[0:8, :] = jnp.zeros((8, w), F32)
        h_ref[...] = jnp.zeros_like(h_ref)

    xg = xgr_ref[0, :, :w]
    xbuf_ref[8:, :] = xgr_ref[0, :, w:]
    xc = cb_ref[...] + xbuf_ref[8:, :] * cw_ref[CONV_WIDTH - 1:CONV_WIDTH, :]
    for k in range(1, CONV_WIDTH):
        xc = xc + xbuf_ref[8 - k:8 - k + ts, :] * cw_ref[CONV_WIDTH - 1 - k:CONV_WIDTH - k, :]
    xbuf_ref[0:8, :] = xbuf_ref[ts:ts + 8, :]

    xcb = xc.astype(BF16)
    r = jax.nn.sigmoid(_dot(xcb, wa_ref[...]) + ba_ref[...])
    i = jax.nn.sigmoid(_dot(xcb, wx_ref[...]) + bx_ref[...])
    nl = -lam_ref[...]
    softplus = jnp.maximum(nl, 0.0) + jnp.log(1.0 + jnp.exp(-jnp.abs(nl)))
    log_a = (-LRU_C * softplus) * r
    a = jnp.exp(log_a)
    bv = jnp.sqrt(1.0 - jnp.exp(2.0 * log_a)) * (i * xc)

    row = lax.broadcasted_iota(jnp.int32, (ts, w), 0)
    sh = 1
    while sh < ts:
        a_prev = jnp.where(row >= sh, pltpu.roll(a, sh, 0), 1.0)
        b_prev = jnp.where(row >= sh, pltpu.roll(bv, sh, 0), 0.0)
        bv = a * b_prev + bv
        a = a * a_prev
        sh *= 2
    h = bv + a * h_ref[0:1, :]
    h_ref[0:1, :] = h[ts - 1:ts, :]
    out_ref[0] = h * jax.nn.gelu(xg)


def _lru(xgr, cw, cb, wa, ba, wx, bx, lam):
    b, s, w2 = xgr.shape
    w = w2 // 2
    ts = LRU_TILE
    return pl.pallas_call(
        _lru_kernel,
        grid=(b, s // ts),
        in_specs=[pl.BlockSpec((1, ts, w2), lambda bi, ti: (bi, ti, 0)),
                  _const_spec(cw.shape), _const_spec(cb.shape), _const_spec(wa.shape), _const_spec(ba.shape),
                  _const_spec(wx.shape), _const_spec(bx.shape), _const_spec(lam.shape)],
        out_specs=pl.BlockSpec((1, ts, w), lambda bi, ti: (bi, ti, 0)),
        out_shape=jax.ShapeDtypeStruct((b, s, w), F32),
        scratch_shapes=[pltpu.VMEM((ts + 8, w), F32), pltpu.VMEM((8, w), F32)],
        compiler_params=_params(("parallel", "arbitrary")),
        name="rg_lru",
    )(xgr, cw, cb, wa, ba, wx, bx, lam)


def _mixout_kernel(x_ref, att_ref, lru_ref, ga_ref, gl_ref, wa_ref, wl_ref, gp_ref, out_ref):
    a = _rms(att_ref[...], ga_ref[...]).astype(BF16)
    l = _rms(lru_ref[...], gl_ref[...]).astype(BF16)
    mixed = _dot(a, wa_ref[...]) + _dot(l, wl_ref[...])
    out_ref[...] = x_ref[...] + _rms(mixed, gp_ref[...])


def _mixout(x2, att, lru, ga, gl, wa, wl, gp):
    n, d = x2.shape
    tm = ROW_TILE
    row = lambda w: pl.BlockSpec((tm, w), lambda i: (i, 0))
    return pl.pallas_call(
        _mixout_kernel,
        grid=(n // tm,),
        in_specs=[row(d), row(att.shape[1]), row(lru.shape[1]), _const_spec(ga.shape), _const_spec(gl.shape),
                  _const_spec(wa.shape), _const_spec(wl.shape), _const_spec(gp.shape)],
        out_specs=row(d),
        out_shape=jax.ShapeDtypeStruct((n, d), F32),
        compiler_params=_params(("parallel",)),
        name="mix_out",
    )(x2, att, lru, ga, gl, wa, wl, gp)


def _memkv_kernel(mem_ref, g_ref, wk_ref, wv_ref, kt_ref, v_ref):
    mn = _rms(mem_ref[0], g_ref[...]).astype(BF16)
    kt_ref[0] = _dot(mn, wk_ref[...]).T.astype(BF16)
    v_ref[0] = _dot(mn, wv_ref[...]).astype(BF16)


def _memkv(mem, g, wk, wv):
    b, m, d = mem.shape
    return pl.pallas_call(
        _memkv_kernel,
        grid=(b,),
        in_specs=[pl.BlockSpec((1, m, d), lambda i: (i, 0, 0)), _const_spec(g.shape),
                  _const_spec(wk.shape), _const_spec(wv.shape)],
        out_specs=[pl.BlockSpec((1, d, m), lambda i: (i, 0, 0)), pl.BlockSpec((1, m, d), lambda i: (i, 0, 0))],
        out_shape=[jax.ShapeDtypeStruct((b, d, m), BF16), jax.ShapeDtypeStruct((b, m, d), BF16)],
        compiler_params=_params(("parallel",)),
        name="mem_kv",
    )(mem, g, wk, wv)


def _xattn_kernel(x_ref, gpre_ref, wq_ref, kt_ref, v_ref, wo_ref, gpost_ref, out_ref):
    x = x_ref[0]
    d = x.shape[1]
    dh = d // X_HEADS
    h = _rms(x, gpre_ref[...]).astype(BF16)
    cq = (_dot(h, wq_ref[...]) * (dh ** -0.5)).astype(BF16)
    outs = []
    for hh in range(X_HEADS):
        hs = slice(hh * dh, (hh + 1) * dh)
        s = _dot(cq[:, hs], kt_ref[0, hs, :])
        e = jnp.exp(s - jnp.max(s, axis=1, keepdims=True))
        p = e / jnp.sum(e, axis=1, keepdims=True)
        outs.append(_dot(p.astype(BF16), v_ref[0, :, hs]).astype(BF16))
    co = _dot(jnp.concatenate(outs, axis=1), wo_ref[...])
    out_ref[0] = x + _rms(co, gpost_ref[...])


def _xattn(x, gpre, wq, kt, v, wo, gpost):
    b, s, d = x.shape
    m = v.shape[1]
    tm = ROW_TILE
    return pl.pallas_call(
        _xattn_kernel,
        grid=(b, s // tm),
        in_specs=[pl.BlockSpec((1, tm, d), lambda bi, i: (bi, i, 0)), _const_spec(gpre.shape),
                  _const_spec(wq.shape),
                  pl.BlockSpec((1, d, m), lambda bi, i: (bi, 0, 0)),
                  pl.BlockSpec((1, m, d), lambda bi, i: (bi, 0, 0)),
                  _const_spec(wo.shape), _const_spec(gpost.shape)],
        out_specs=pl.BlockSpec((1, tm, d), lambda bi, i: (bi, i, 0)),
        out_shape=jax.ShapeDtypeStruct((b, s, d), F32),
        compiler_params=_params(("parallel", "parallel")),
        name="mem_xattn",
    )(x, gpre, wq, kt, v, wo, gpost)


def _mlp_kernel(x_ref, gpre_ref, w1_ref, w2_ref, gpost_ref, out_ref):
    x = x_ref[...]
    h = _rms(x, gpre_ref[...]).astype(BF16)
    dff = w1_ref.shape[1]
    ck = 1024
    acc = jnp.zeros(x.shape, F32)
    for c in range(dff // ck):
        u = jnp.maximum(_dot(h, w1_ref[:, c * ck:(c + 1) * ck]), 0.0)
        acc = acc + _dot((u * u).astype(BF16), w2_ref[c * ck:(c + 1) * ck, :])
    out_ref[...] = x + _rms(acc, gpost_ref[...])


def _mlp(x2, gpre, w1, w2, gpost):
    n, d = x2.shape
    tm = ROW_TILE
    row = pl.BlockSpec((tm, d), lambda i: (i, 0))
    return pl.pallas_call(
        _mlp_kernel,
        grid=(n // tm,),
        in_specs=[row, _const_spec(gpre.shape), _const_spec(w1.shape), _const_spec(w2.shape),
                  _const_spec(gpost.shape)],
        out_specs=row,
        out_shape=jax.ShapeDtypeStruct((n, d), F32),
        compiler_params=_params(("parallel",)),
        name="mlp",
    )(x2, gpre, w1, w2, gpost)


def _block_structured(w):
    eye = jnp.eye(NSA_KV_HEADS, dtype=w.dtype)
    big = w[:, None, :, None, :] * eye[None, :, None, :, None]
    return big.reshape(CMP_BLOCK * KV_WIDTH, KV_WIDTH).astype(BF16)


def _block_diag(w):
    n, d, e = w.shape
    eye = jnp.eye(n, dtype=w.dtype)
    return (w[:, :, None, :] * eye[:, None, :, None]).reshape(n * d, n * e).astype(BF16)


def kernel(x, mem, rel_bias, ln_mix_pre, ln_mix_post, w_in, cmp_pe_k, cmp_pe_v, cmp_w_k, cmp_w_v,
           conv_w, conv_b, lru_wa, lru_ba, lru_wx, lru_bx, lru_lambda, gn_attn, gn_lru, w_out,
           ln_x_pre, ln_x_post, ln_mem, xq, xkv, xo, ln_mlp_pre, ln_mlp_post, mlp_w1, mlp_w2):
    b, s, d = x.shape
    depth = w_in.shape[0]
    n = b * s
    lru_w = conv_w.shape[2]
    assert s % TQ == 0 and s % LRU_TILE == 0 and n % ROW_TILE == 0
    assert s // CMP_BLOCK == 128, "selection mask layout assumes 128 compression blocks"
    row1 = lambda v: v.reshape(1, -1)

    tabs = _bias_tables(rel_bias)
    x2 = x.reshape(n, d)
    o_q = NSA_WIDTH
    o_g = o_q + 6 * KV_WIDTH
    o_x = o_g + 3 * NSA_HEADS
    for l in range(depth):
        wl = w_in[l]
        wq = wl[:, :o_q].astype(BF16)
        wc = wl[:, o_q:o_q + 2 * KV_WIDTH].astype(BF16)
        wkv = wl[:, o_q + 2 * KV_WIDTH:o_g].astype(BF16)
        wg = jnp.pad(wl[:, o_g:o_x], ((0, 0), (0, 128 - 3 * NSA_HEADS))).astype(BF16)
        wx = wl[:, o_x:].astype(BF16)
        q, kcr, vcr, kv, gt, xgr = _inproj(x2, row1(ln_mix_pre[l]), wq, wc, wkv, wg, wx)

        nblk = n // CMP_BLOCK
        pek = jnp.tile(cmp_pe_k[l], (1, NSA_KV_HEADS)).reshape(1, -1)
        pev = jnp.tile(cmp_pe_v[l], (1, NSA_KV_HEADS)).reshape(1, -1)
        kc, vc = _compress(kcr.reshape(nblk, CMP_BLOCK * KV_WIDTH), vcr.reshape(nblk, CMP_BLOCK * KV_WIDTH),
                           pek, pev, _block_structured(cmp_w_k[l]), _block_structured(cmp_w_v[l]))

        att = _attention(q.reshape(b, s, NSA_WIDTH), gt.reshape(b, s, 128),
                         kc.reshape(b, s // CMP_BLOCK, KV_WIDTH), vc.reshape(b, s // CMP_BLOCK, KV_WIDTH),
                         kv.reshape(b, s, 4 * KV_WIDTH), tabs)
        lru = _lru(xgr.reshape(b, s, 2 * lru_w), conv_w[l], row1(conv_b[l]),
                   _block_diag(lru_wa[l]), row1(lru_ba[l]), _block_diag(lru_wx[l]), row1(lru_bx[l]),
                   row1(lru_lambda[l]))
        wo = w_out[l].astype(BF16)
        x2 = _mixout(x2, att.reshape(n, NSA_WIDTH), lru.reshape(n, lru_w), row1(gn_attn[l]), row1(gn_lru[l]),
                     wo[:NSA_WIDTH], wo[NSA_WIDTH:], row1(ln_mix_post[l]))

        wkv_x = xkv[l].astype(BF16)
        kt, v = _memkv(mem, row1(ln_mem[l]), wkv_x[:, :d], wkv_x[:, d:])
        x2 = _xattn(x2.reshape(b, s, d), row1(ln_x_pre[l]), xq[l].astype(BF16), kt, v, xo[l].astype(BF16),
                    row1(ln_x_post[l])).reshape(n, d)

        x2 = _mlp(x2, row1(ln_mlp_pre[l]), mlp_w1[l].astype(BF16), mlp_w2[l].astype(BF16),
                  row1(ln_mlp_post[l]))
    return x2.reshape(b, s, d)
```

```python
import functools
import math

import numpy as np
import jax
import jax.numpy as jnp
from jax import lax
from jax.experimental import pallas as pl
from jax.experimental.pallas import tpu as pltpu

HEAD_DIM = 64
NSA_HEADS = 8
NSA_KV_HEADS = 2
NSA_HPG = NSA_HEADS // NSA_KV_HEADS
NSA_WIDTH = NSA_HEADS * HEAD_DIM
KV_WIDTH = NSA_KV_HEADS * HEAD_DIM
CMP_BLOCK = 64
SEL_TOPN = 16
WINDOW = 512
FORCE_SCORE = 1e4
LRU_BLOCKS = 8
CONV_WIDTH = 4
LRU_C = 8.0
X_HEADS = 4
N_BUCKETS = 32
MAX_DISTANCE = 128
EPS = 1e-6
NEG = -1e30

TQ = 256
CK = 256
ROWS = NSA_HPG * TQ
VT_ROWS = HEAD_DIM + 16
LOG2E = 1.4426950408889634
MASKV = -(2.0 ** 100)
M_INIT = -1e30
ROW_TILE = 512
LRU_TILE = 512
VMEM_LIMIT = 56 * 1024 * 1024

F32 = jnp.float32
BF16 = jnp.bfloat16


def _bucket_of_distance(d):
    max_exact = N_BUCKETS // 2
    d = np.maximum(d, 0)
    df = np.maximum(d, 1).astype(np.float64)
    large = max_exact + (np.log(df / max_exact) / math.log(MAX_DISTANCE / max_exact)
                         * (N_BUCKETS - max_exact)).astype(np.int32)
    large = np.minimum(large, N_BUCKETS - 1)
    return np.where(d < max_exact, d, large).astype(np.int32)


def _rms(x, g):
    return x * lax.rsqrt(jnp.mean(x * x, axis=-1, keepdims=True) + EPS) * g


def _dot(a, b):
    return jnp.dot(a, b, preferred_element_type=F32)


def _params(sem):
    return pltpu.CompilerParams(dimension_semantics=sem, vmem_limit_bytes=VMEM_LIMIT)


def _const_spec(shape):
    nd = len(shape)
    return pl.BlockSpec(shape, lambda *_: (0,) * nd)


def _inproj_kernel(x_ref, g_ref, wq_ref, wc_ref, wkv_ref, wg_ref, wx_ref,
                   q_ref, kcr_ref, vcr_ref, kv_ref, gt_ref, xgr_ref):
    h = _rms(x_ref[...], g_ref[...]).astype(BF16)
    q_ref[...] = (_dot(h, wq_ref[...]) * (HEAD_DIM ** -0.5 * LOG2E)).astype(BF16)
    c = _dot(h, wc_ref[...])
    kcr_ref[...] = c[:, :KV_WIDTH]
    vcr_ref[...] = c[:, KV_WIDTH:]
    kv_ref[...] = _dot(h, wkv_ref[...]).astype(BF16)
    gt_ref[...] = _dot(h, wg_ref[...])
    xgr_ref[...] = _dot(h, wx_ref[...])


def _inproj(x2, gain, wq, wc, wkv, wg, wx):
    n, d = x2.shape
    tm = ROW_TILE
    row = lambda w: pl.BlockSpec((tm, w), lambda i: (i, 0))
    return pl.pallas_call(
        _inproj_kernel,
        grid=(n // tm,),
        in_specs=[row(d), _const_spec(gain.shape), _const_spec(wq.shape), _const_spec(wc.shape),
                  _const_spec(wkv.shape), _const_spec(wg.shape), _const_spec(wx.shape)],
        out_specs=[row(NSA_WIDTH), row(KV_WIDTH), row(KV_WIDTH), row(4 * KV_WIDTH), row(128),
                   row(wx.shape[1])],
        out_shape=[jax.ShapeDtypeStruct((n, NSA_WIDTH), BF16),
                   jax.ShapeDtypeStruct((n, KV_WIDTH), F32),
                   jax.ShapeDtypeStruct((n, KV_WIDTH), F32),
                   jax.ShapeDtypeStruct((n, 4 * KV_WIDTH), BF16),
                   jax.ShapeDtypeStruct((n, 128), F32),
                   jax.ShapeDtypeStruct((n, wx.shape[1]), F32)],
        compiler_params=_params(("parallel",)),
        name="inproj",
    )(x2, gain, wq, wc, wkv, wg, wx)


def _compress_kernel(kr_ref, vr_ref, pek_ref, pev_ref, wk_ref, wv_ref, kc_ref, vc_ref):
    kc_ref[...] = _dot((kr_ref[...] + pek_ref[...]).astype(BF16), wk_ref[...]).astype(BF16)
    vc_ref[...] = _dot((vr_ref[...] + pev_ref[...]).astype(BF16), wv_ref[...]).astype(BF16)


def _compress(kr, vr, pek, pev, wk, wv):
    nblk, width = kr.shape
    tm = 128
    row = lambda w: pl.BlockSpec((tm, w), lambda i: (i, 0))
    return pl.pallas_call(
        _compress_kernel,
        grid=(nblk // tm,),
        in_specs=[row(width), row(width), _const_spec(pek.shape), _const_spec(pev.shape),
                  _const_spec(wk.shape), _const_spec(wv.shape)],
        out_specs=[row(KV_WIDTH), row(KV_WIDTH)],
        out_shape=[jax.ShapeDtypeStruct((nblk, KV_WIDTH), BF16)] * 2,
        compiler_params=_params(("parallel",)),
        name="compress",
    )(kr, vr, pek, pev, wk, wv)


def _bias_tables_kernel(rb_ref, bma_ref, bmb_ref, bm1_ref, bm2_ref,
                        ta_ref, tb_ref, d1_ref, d2_ref):
    last = N_BUCKETS - 1
    for hd in range(NSA_HEADS):
        g, hl = divmod(hd, NSA_HPG)
        cols = slice(hl * TQ, (hl + 1) * TQ)
        far = rb_ref[last, hd]
        for bm_ref, out_ref, masked in ((bma_ref, ta_ref, False), (bmb_ref, tb_ref, True),
                                        (bm1_ref, d1_ref, False), (bm2_ref, d2_ref, False)):
            bm = bm_ref[...]
            acc = jnp.zeros(bm.shape, F32)
            for k in range(N_BUCKETS - 1):
                acc = jnp.where(bm == k, (rb_ref[k, hd] - far) * LOG2E, acc)
            if masked:
                acc = jnp.where(bm < 0, MASKV, acc)
            out_ref[g, :, cols] = acc


def _bias_tables(rel_bias):
    ki = np.arange(CK)[:, None]
    qi = np.arange(TQ)[None, :]
    bma = _bucket_of_distance(qi - ki + CK)
    bmb = np.where(qi >= ki, _bucket_of_distance(qi - ki), -1).astype(np.int32)
    r = (np.arange(TQ) + 1) % CMP_BLOCK
    bm1 = np.broadcast_to(_bucket_of_distance(r)[None, :], (8, TQ)).astype(np.int32)
    bm2 = np.broadcast_to(_bucket_of_distance(r + CMP_BLOCK)[None, :], (8, TQ)).astype(np.int32)
    return pl.pallas_call(
        _bias_tables_kernel,
        in_specs=[pl.BlockSpec(memory_space=pltpu.SMEM)] + [pl.BlockSpec(memory_space=pltpu.VMEM)] * 4,
        out_specs=[pl.BlockSpec(memory_space=pltpu.VMEM)] * 4,
        out_shape=[jax.ShapeDtypeStruct((NSA_KV_HEADS, CK, ROWS), F32),
                   jax.ShapeDtypeStruct((NSA_KV_HEADS, CK, ROWS), F32),
                   jax.ShapeDtypeStruct((NSA_KV_HEADS, 8, ROWS), F32),
                   jax.ShapeDtypeStruct((NSA_KV_HEADS, 8, ROWS), F32)],
        name="bias_tables",
    )(rel_bias, jnp.asarray(bma), jnp.asarray(bmb), jnp.asarray(bm1), jnp.asarray(bm2))


def _attend(chains):
    scores = []
    for chunks, _, _ in chains:
        row = []
        for keys, q, bias, _ in chunks:
            s = _dot(keys, q)
            row.append(s if bias is None else s + bias)
        scores.append(row)
    for (chunks, m_ref, acc_ref), row in zip(chains, scores):
        m = m_ref[...]
        acc = acc_ref[...]
        for (_, _, _, vt), s in zip(chunks, row):
            m_new = jnp.maximum(m, jnp.max(s, axis=0, keepdims=True))
            p = jnp.exp2(s - m_new).astype(BF16)
            acc = jnp.exp2(m - m_new) * acc + _dot(vt, p)
            m = m_new
        m_ref[...] = m
        acc_ref[...] = acc


def _attn_kernel(q_ref, gt_ref, kc_ref, vc_ref, ks_ref, vs_ref, kw_ref, vw_ref,
                 ta_ref, tb_ref, d1_ref, d2_ref, out_ref,
                 ka_ref, vst_ref, vwt_ref, vct_ref, qa_ref, ms_ref, accs_ref, mw_ref, accw_ref, outt_ref):
    qt = pl.program_id(1)
    n_blocks = kc_ref.shape[1]
    blocks_per_tile = TQ // CMP_BLOCK

    @pl.when(qt == 0)
    def _():
        ones_rows = jnp.where(lax.broadcasted_iota(jnp.int32, (VT_ROWS - HEAD_DIM, CK), 0) == 0,
                              1.0, 0.0).astype(BF16)

        def build(c, carry):
            rs = pl.ds(pl.multiple_of(c * CK, CK), CK)
            blk = lax.broadcasted_iota(jnp.int32, (CK, 128), 1)
            key_blk = c * (CK // CMP_BLOCK) + (lax.broadcasted_iota(jnp.int32, (CK, 128), 0) >> 6)
            ka_ref[rs, 0:128] = jnp.where(blk == key_blk, 1.0, 0.0).astype(BF16)
            ka_ref[rs, 128:256] = ks_ref[0, rs, :]
            vs_t = vs_ref[0, rs, :].astype(F32).T
            vw_t = vw_ref[0, rs, :].astype(F32).T
            for g in range(NSA_KV_HEADS):
                gs = slice(g * HEAD_DIM, (g + 1) * HEAD_DIM)
                vst_ref[c, g, 0:HEAD_DIM, :] = vs_t[gs].astype(BF16)
                vst_ref[c, g, HEAD_DIM:, :] = ones_rows
                vwt_ref[c, g, 0:HEAD_DIM, :] = vw_t[gs].astype(BF16)
                vwt_ref[c, g, HEAD_DIM:, :] = ones_rows
            return carry
        lax.fori_loop(0, ks_ref.shape[1] // CK, build, 0)
        vct_ref[...] = vc_ref[0].astype(F32).T.astype(BF16)

    q_t = q_ref[0].astype(F32).T
    gates_t = jax.nn.sigmoid(gt_ref[0]).T
    kc = kc_ref[0]

    q_loc = lax.broadcasted_iota(jnp.int32, (1, ROWS), 1) & (TQ - 1)
    n_iota = lax.broadcasted_iota(jnp.int32, (n_blocks, ROWS), 0)
    n1 = qt * blocks_per_tile + ((q_loc + 1) >> 6) - 1
    j_s = lax.broadcasted_iota(jnp.int32, (n_blocks, TQ), 0)
    j_f = j_s.astype(F32)
    cur = qt * blocks_per_tile + (lax.broadcasted_iota(jnp.int32, (1, TQ), 1) >> 6)
    zeros_half = jnp.zeros((HEAD_DIM, ROWS), BF16)

    def chunk_rows(c):
        return pl.ds(pl.multiple_of(c * CK, CK), CK)

    for g in range(NSA_KV_HEADS):
        gs = slice(g * HEAD_DIM, (g + 1) * HEAD_DIM)
        q_g = jnp.concatenate(
            [q_t[(g * NSA_HPG + hl) * HEAD_DIM:(g * NSA_HPG + hl + 1) * HEAD_DIM, :]
             for hl in range(NSA_HPG)], axis=1).astype(BF16)
        q_both = jnp.concatenate([q_g, zeros_half] if g == 0 else [zeros_half, q_g], axis=0)

        s_c = _dot(kc, q_both)
        bias_c = (jnp.where(n_iota == n1, d1_ref[g, 0:1, :], 0.0)
                  + jnp.where(n_iota == n1 - 1, d2_ref[g, 0:1, :], 0.0))
        valid_c = n_iota <= n1
        l_c = jnp.where(valid_c, s_c + bias_c, NEG)
        m_c = jnp.max(l_c, axis=0, keepdims=True)
        e_c = jnp.where(valid_c, jnp.exp2(l_c - m_c), 0.0)
        p_c = e_c / jnp.maximum(jnp.sum(e_c, axis=0, keepdims=True), 1e-30)
        o_c = _dot(vct_ref[gs, :], p_c.astype(BF16))

        imp = p_c[:, 0:TQ]
        for hl in range(1, NSA_HPG):
            imp = imp + p_c[:, hl * TQ:(hl + 1) * TQ]
        forced = (j_s == 0) | (j_s == cur) | (j_s == cur - 1)
        work = jnp.where(j_s <= cur, jnp.where(forced, FORCE_SCORE, imp), NEG)
        chosen = jnp.zeros((n_blocks, TQ), F32)
        for _ in range(min(SEL_TOPN, n_blocks)):
            mx = jnp.max(work, axis=0, keepdims=True)
            first = jnp.min(jnp.where(work == mx, j_f, float(n_blocks)), axis=0, keepdims=True)
            pick = j_f == first
            chosen = jnp.where(pick, jnp.where(mx > NEG / 2, 1.0, 0.0), chosen)
            work = jnp.where(pick, -jnp.inf, work)
        sel_mask = jnp.where(chosen > 0.5, 0.0, MASKV).astype(BF16)
        qa_ref[0:128, :] = jnp.concatenate([sel_mask] * NSA_HPG, axis=1)
        qa_ref[128:256, :] = q_both

        for m_ref, acc_ref in ((ms_ref, accs_ref), (mw_ref, accw_ref)):
            m_ref[...] = jnp.full(m_ref.shape, M_INIT, F32)
            acc_ref[...] = jnp.zeros(acc_ref.shape, F32)

        def sel(c, bias=None):
            return (ka_ref[chunk_rows(c), :], qa_ref[...], bias, vst_ref[c, g])

        def win(c, bias):
            return (kw_ref[0, chunk_rows(c), :], q_both, bias, vwt_ref[c, g])

        n_far = jnp.maximum(qt - 1, 0)

        def far4(i, carry):
            _attend([([sel(4 * i + k) for k in range(4)], ms_ref, accs_ref)])
            return carry
        lax.fori_loop(0, n_far >> 2, far4, 0)
        done4 = (n_far >> 2) << 2

        @pl.when((n_far & 2) != 0)
        def _():
            _attend([([sel(done4), sel(done4 + 1)], ms_ref, accs_ref)])

        @pl.when((n_far & 1) != 0)
        def _():
            _attend([([sel(done4 + (n_far & 2))], ms_ref, accs_ref)])

        @pl.when(qt >= 2)
        def _():
            w2_mask = jnp.where(q_loc < lax.broadcasted_iota(jnp.int32, (CK, ROWS), 0), 0.0, MASKV)
            ta, tb = ta_ref[g], tb_ref[g]
            _attend([([sel(qt - 1, ta), sel(qt, tb)], ms_ref, accs_ref),
                     ([win(qt - 2, w2_mask), win(qt - 1, ta), win(qt, tb)], mw_ref, accw_ref)])

        @pl.when(qt == 1)
        def _():
            ta, tb = ta_ref[g], tb_ref[g]
            _attend([([sel(0, ta), sel(1, tb)], ms_ref, accs_ref),
                     ([win(0, ta), win(1, tb)], mw_ref, accw_ref)])

        @pl.when(qt == 0)
        def _():
            tb = tb_ref[g]
            _attend([([sel(0, tb)], ms_ref, accs_ref), ([win(0, tb)], mw_ref, accw_ref)])

        o_s = accs_ref[0:HEAD_DIM, :] / accs_ref[HEAD_DIM:HEAD_DIM + 1, :]
        o_w = accw_ref[0:HEAD_DIM, :] / accw_ref[HEAD_DIM:HEAD_DIM + 1, :]

        def gate(branch):
            return jnp.concatenate(
                [gates_t[(g * NSA_HPG + hl) * 3 + branch:(g * NSA_HPG + hl) * 3 + branch + 1, :]
                 for hl in range(NSA_HPG)], axis=1)
        mixed = gate(0) * o_c + gate(1) * o_s + gate(2) * o_w
        for hl in range(NSA_HPG):
            hd = g * NSA_HPG + hl
            outt_ref[hd * HEAD_DIM:(hd + 1) * HEAD_DIM, :] = mixed[:, hl * TQ:(hl + 1) * TQ]

    out_ref[0] = outt_ref[...].T


def _attention(q, gates, kc, vc, kv, tabs):
    b, s, _ = q.shape
    n_blocks = s // CMP_BLOCK
    nch = s // CK
    ta, tb, d1, d2 = tabs
    tile = lambda w: pl.BlockSpec((1, TQ, w), lambda bi, qi: (bi, qi, 0))
    per_b = lambda r, w, col: pl.BlockSpec((1, r, w), lambda bi, qi: (bi, 0, col))
    return pl.pallas_call(
        _attn_kernel,
        grid=(b, s // TQ),
        in_specs=[tile(NSA_WIDTH), tile(128),
                  per_b(n_blocks, KV_WIDTH, 0), per_b(n_blocks, KV_WIDTH, 0),
                  per_b(s, KV_WIDTH, 0), per_b(s, KV_WIDTH, 1), per_b(s, KV_WIDTH, 2), per_b(s, KV_WIDTH, 3),
                  _const_spec(ta.shape), _const_spec(tb.shape), _const_spec(d1.shape), _const_spec(d2.shape)],
        out_specs=tile(NSA_WIDTH),
        out_shape=jax.ShapeDtypeStruct((b, s, NSA_WIDTH), F32),
        scratch_shapes=[pltpu.VMEM((s, CK), BF16),
                        pltpu.VMEM((nch, NSA_KV_HEADS, VT_ROWS, CK), BF16),
                        pltpu.VMEM((nch, NSA_KV_HEADS, VT_ROWS, CK), BF16),
                        pltpu.VMEM((KV_WIDTH, n_blocks), BF16),
                        pltpu.VMEM((CK, ROWS), BF16),
                        pltpu.VMEM((1, ROWS), F32), pltpu.VMEM((VT_ROWS, ROWS), F32),
                        pltpu.VMEM((1, ROWS), F32), pltpu.VMEM((VT_ROWS, ROWS), F32),
                        pltpu.VMEM((NSA_WIDTH, TQ), F32)],
        compiler_params=_params(("parallel", "arbitrary")),
        name="nsa_attention",
    )(q, gates, kc, vc, kv, kv, kv, kv, ta, tb, d1, d2)


def _lru_kernel(xgr_ref, cw_ref, cb_ref, wa_ref, ba_ref, wx_ref, bx_ref, lam_ref,
                out_ref, xbuf_ref, h_ref):
    t = pl.program_id(1)
    ts = LRU_TILE
    w = out_ref.shape[2]

    @pl.when(t == 0)
    def _():
        xbuf_ref[0:8, :] = jnp.zeros((8, w), F32)
        h_ref[...] = jnp.zeros_like(h_ref)

    xg = xgr_ref[0, :, :w]
    xbuf_ref[8:, :] = xgr_ref[0, :, w:]
    xc = cb_ref[...] + xbuf_ref[8:, :] * cw_ref[CONV_WIDTH - 1:CONV_WIDTH, :]
    for k in range(1, CONV_WIDTH):
        xc = xc + xbuf_ref[8 - k:8 - k + ts, :] * cw_ref[CONV_WIDTH - 1 - k:CONV_WIDTH - k, :]
    xbuf_ref[0:8, :] = xbuf_ref[ts:ts + 8, :]

    xcb = xc.astype(BF16)
    r = jax.nn.sigmoid(_dot(xcb, wa_ref[...]) + ba_ref[...])
    i = jax.nn.sigmoid(_dot(xcb, wx_ref[...]) + bx_ref[...])
    nl = -lam_ref[...]
    softplus = jnp.maximum(nl, 0.0) + jnp.log(1.0 + jnp.exp(-jnp.abs(nl)))
    log_a = (-LRU_C * softplus) * r
    a = jnp.exp(log_a)
    bv = jnp.sqrt(1.0 - jnp.exp(2.0 * log_a)) * (i * xc)

    row = lax.broadcasted_iota(jnp.int32, (ts, w), 0)
    sh = 1
    while sh < ts:
        a_prev = jnp.where(row >= sh, pltpu.roll(a, sh, 0), 1.0)
        b_prev = jnp.where(row >= sh, pltpu.roll(bv, sh, 0), 0.0)
        bv = a * b_prev + bv
        a = a * a_prev
        sh *= 2
    h = bv + a * h_ref[0:1, :]
    h_ref[0:1, :] = h[ts - 1:ts, :]
    out_ref[0] = h * jax.nn.gelu(xg)


def _lru(xgr, cw, cb, wa, ba, wx, bx, lam):
    b, s, w2 = xgr.shape
    w = w2 // 2
    ts = LRU_TILE
    return pl.pallas_call(
        _lru_kernel,
        grid=(b, s // ts),
        in_specs=[pl.BlockSpec((1, ts, w2), lambda bi, ti: (bi, ti, 0)),
                  _const_spec(cw.shape), _const_spec(cb.shape), _const_spec(wa.shape), _const_spec(ba.shape),
                  _const_spec(wx.shape), _const_spec(bx.shape), _const_spec(lam.shape)],
        out_specs=pl.BlockSpec((1, ts, w), lambda bi, ti: (bi, ti, 0)),
        out_shape=jax.ShapeDtypeStruct((b, s, w), F32),
        scratch_shapes=[pltpu.VMEM((ts + 8, w), F32), pltpu.VMEM((8, w), F32)],
        compiler_params=_params(("parallel", "arbitrary")),
        name="rg_lru",
    )(xgr, cw, cb, wa, ba, wx, bx, lam)


def _mixout_kernel(x_ref, att_ref, lru_ref, ga_ref, gl_ref, wa_ref, wl_ref, gp_ref, out_ref):
    a = _rms(att_ref[...], ga_ref[...]).astype(BF16)
    l = _rms(lru_ref[...], gl_ref[...]).astype(BF16)
    mixed = _dot(a, wa_ref[...]) + _dot(l, wl_ref[...])
    out_ref[...] = x_ref[...] + _rms(mixed, gp_ref[...])


def _mixout(x2, att, lru, ga, gl, wa, wl, gp):
    n, d = x2.shape
    tm = ROW_TILE
    row = lambda w: pl.BlockSpec((tm, w), lambda i: (i, 0))
    return pl.pallas_call(
        _mixout_kernel,
        grid=(n // tm,),
        in_specs=[row(d), row(att.shape[1]), row(lru.shape[1]), _const_spec(ga.shape), _const_spec(gl.shape),
                  _const_spec(wa.shape), _const_spec(wl.shape), _const_spec(gp.shape)],
        out_specs=row(d),
        out_shape=jax.ShapeDtypeStruct((n, d), F32),
        compiler_params=_params(("parallel",)),
        name="mix_out",
    )(x2, att, lru, ga, gl, wa, wl, gp)


def _memkv_kernel(mem_ref, g_ref, wk_ref, wv_ref, kt_ref, v_ref):
    mn = _rms(mem_ref[0], g_ref[...]).astype(BF16)
    kt_ref[0] = _dot(mn, wk_ref[...]).T.astype(BF16)
    v_ref[0] = _dot(mn, wv_ref[...]).astype(BF16)


def _memkv(mem, g, wk, wv):
    b, m, d = mem.shape
    return pl.pallas_call(
        _memkv_kernel,
        grid=(b,),
        in_specs=[pl.BlockSpec((1, m, d), lambda i: (i, 0, 0)), _const_spec(g.shape),
                  _const_spec(wk.shape), _const_spec(wv.shape)],
        out_specs=[pl.BlockSpec((1, d, m), lambda i: (i, 0, 0)), pl.BlockSpec((1, m, d), lambda i: (i, 0, 0))],
        out_shape=[jax.ShapeDtypeStruct((b, d, m), BF16), jax.ShapeDtypeStruct((b, m, d), BF16)],
        compiler_params=_params(("parallel",)),
        name="mem_kv",
    )(mem, g, wk, wv)


def _xattn_kernel(x_ref, gpre_ref, wq_ref, kt_ref, v_ref, wo_ref, gpost_ref, out_ref):
    x = x_ref[0]
    d = x.shape[1]
    dh = d // X_HEADS
    h = _rms(x, gpre_ref[...]).astype(BF16)
    cq = (_dot(h, wq_ref[...]) * (dh ** -0.5)).astype(BF16)
    outs = []
    for hh in range(X_HEADS):
        hs = slice(hh * dh, (hh + 1) * dh)
        s = _dot(cq[:, hs], kt_ref[0, hs, :])
        e = jnp.exp(s - jnp.max(s, axis=1, keepdims=True))
        p = e / jnp.sum(e, axis=1, keepdims=True)
        outs.append(_dot(p.astype(BF16), v_ref[0, :, hs]).astype(BF16))
    co = _dot(jnp.concatenate(outs, axis=1), wo_ref[...])
    out_ref[0] = x + _rms(co, gpost_ref[...])


def _xattn(x, gpre, wq, kt, v, wo, gpost):
    b, s, d = x.shape
    m = v.shape[1]
    tm = ROW_TILE
    return pl.pallas_call(
        _xattn_kernel,
        grid=(b, s // tm),
        in_specs=[pl.BlockSpec((1, tm, d), lambda bi, i: (bi, i, 0)), _const_spec(gpre.shape),
                  _const_spec(wq.shape),
                  pl.BlockSpec((1, d, m), lambda bi, i: (bi, 0, 0)),
                  pl.BlockSpec((1, m, d), lambda bi, i: (bi, 0, 0)),
                  _const_spec(wo.shape), _const_spec(gpost.shape)],
        out_specs=pl.BlockSpec((1, tm, d), lambda bi, i: (bi, i, 0)),
        out_shape=jax.ShapeDtypeStruct((b, s, d), F32),
        compiler_params=_params(("parallel", "parallel")),
        name="mem_xattn",
    )(x, gpre, wq, kt, v, wo, gpost)


def _mlp_kernel(x_ref, gpre_ref, w1_ref, w2_ref, gpost_ref, out_ref):
    x = x_ref[...]
    h = _rms(x, gpre_ref[...]).astype(BF16)
    dff = w1_ref.shape[1]
    ck = 1024
    acc = jnp.zeros(x.shape, F32)
    for c in range(dff // ck):
        u = jnp.maximum(_dot(h, w1_ref[:, c * ck:(c + 1) * ck]), 0.0)
        acc = acc + _dot((u * u).astype(BF16), w2_ref[c * ck:(c + 1) * ck, :])
    out_ref[...] = x + _rms(acc, gpost_ref[...])


def _mlp(x2, gpre, w1, w2, gpost):
    n, d = x2.shape
    tm = ROW_TILE
    row = pl.BlockSpec((tm, d), lambda i: (i, 0))
    return pl.pallas_call(
        _mlp_kernel,
        grid=(n // tm,),
        in_specs=[row, _const_spec(gpre.shape), _const_spec(w1.shape), _const_spec(w2.shape),
                  _const_spec(gpost.shape)],
        out_specs=row,
        out_shape=jax.ShapeDtypeStruct((n, d), F32),
        compiler_params=_params(("parallel",)),
        name="mlp",
    )(x2, gpre, w1, w2, gpost)


def _block_structured(w):
    eye = jnp.eye(NSA_KV_HEADS, dtype=w.dtype)
    big = w[:, None, :, None, :] * eye[None, :, None, :, None]
    return big.reshape(CMP_BLOCK * KV_WIDTH, KV_WIDTH).astype(BF16)


def _block_diag(w):
    n, d, e = w.shape
    eye = jnp.eye(n, dtype=w.dtype)
    return (w[:, :, None, :] * eye[:, None, :, None]).reshape(n * d, n * e).astype(BF16)


def kernel(x, mem, rel_bias, ln_mix_pre, ln_mix_post, w_in, cmp_pe_k, cmp_pe_v, cmp_w_k, cmp_w_v,
           conv_w, conv_b, lru_wa, lru_ba, lru_wx, lru_bx, lru_lambda, gn_attn, gn_lru, w_out,
           ln_x_pre, ln_x_post, ln_mem, xq, xkv, xo, ln_mlp_pre, ln_mlp_post, mlp_w1, mlp_w2):
    b, s, d = x.shape
    depth = w_in.shape[0]
    n = b * s
    lru_w = conv_w.shape[2]
    assert s % TQ == 0 and s % LRU_TILE == 0 and n % ROW_TILE == 0
    assert s // CMP_BLOCK == 128, "selection mask layout assumes 128 compression blocks"
    row1 = lambda v: v.reshape(1, -1)

    tabs = _bias_tables(rel_bias)
    x2 = x.reshape(n, d)
    o_q = NSA_WIDTH
    o_g = o_q + 6 * KV_WIDTH
    o_x = o_g + 3 * NSA_HEADS
    for l in range(depth):
        wl = w_in[l]
        wq = wl[:, :o_q].astype(BF16)
        wc = wl[:, o_q:o_q + 2 * KV_WIDTH].astype(BF16)
        wkv = wl[:, o_q + 2 * KV_WIDTH:o_g].astype(BF16)
        wg = jnp.pad(wl[:, o_g:o_x], ((0, 0), (0, 128 - 3 * NSA_HEADS))).astype(BF16)
        wx = wl[:, o_x:].astype(BF16)
        q, kcr, vcr, kv, gt, xgr = _inproj(x2, row1(ln_mix_pre[l]), wq, wc, wkv, wg, wx)

        nblk = n // CMP_BLOCK
        pek = jnp.tile(cmp_pe_k[l], (1, NSA_KV_HEADS)).reshape(1, -1)
        pev = jnp.tile(cmp_pe_v[l], (1, NSA_KV_HEADS)).reshape(1, -1)
        kc, vc = _compress(kcr.reshape(nblk, CMP_BLOCK * KV_WIDTH), vcr.reshape(nblk, CMP_BLOCK * KV_WIDTH),
                           pek, pev, _block_structured(cmp_w_k[l]), _block_structured(cmp_w_v[l]))

        att = _attention(q.reshape(b, s, NSA_WIDTH), gt.reshape(b, s, 128),
                         kc.reshape(b, s // CMP_BLOCK, KV_WIDTH), vc.reshape(b, s // CMP_BLOCK, KV_WIDTH),
                         kv.reshape(b, s, 4 * KV_WIDTH), tabs)
        lru = _lru(xgr.reshape(b, s, 2 * lru_w), conv_w[l], row1(conv_b[l]),
                   _block_diag(lru_wa[l]), row1(lru_ba[l]), _block_diag(lru_wx[l]), row1(lru_bx[l]),
                   row1(lru_lambda[l]))
        wo = w_out[l].astype(BF16)
        x2 = _mixout(x2, att.reshape(n, NSA_WIDTH), lru.reshape(n, lru_w), row1(gn_attn[l]), row1(gn_lru[l]),
                     wo[:NSA_WIDTH], wo[NSA_WIDTH:], row1(ln_mix_post[l]))

        wkv_x = xkv[l].astype(BF16)
        kt, v = _memkv(mem, row1(ln_mem[l]), wkv_x[:, :d], wkv_x[:, d:])
        x2 = _xattn(x2.reshape(b, s, d), row1(ln_x_pre[l]), xq[l].astype(BF16), kt, v, xo[l].astype(BF16),
                    row1(ln_x_post[l])).reshape(n, d)

        x2 = _mlp(x2, row1(ln_mlp_pre[l]), mlp_w1[l].astype(BF16), mlp_w2[l].astype(BF16),
                  row1(ln_mlp_post[l]))
    return x2.reshape(b, s, d)
```

```python
import functools
import math

import numpy as np
import jax
import jax.numpy as jnp
from jax import lax
from jax.experimental import pallas as pl
from jax.experimental.pallas import tpu as pltpu

HEAD_DIM = 64
NSA_HEADS = 8
NSA_KV_HEADS = 2
NSA_HPG = NSA_HEADS // NSA_KV_HEADS
NSA_WIDTH = NSA_HEADS * HEAD_DIM
KV_WIDTH = NSA_KV_HEADS * HEAD_DIM
CMP_BLOCK = 64
SEL_TOPN = 16
WINDOW = 512
FORCE_SCORE = 1e4
LRU_BLOCKS = 8
CONV_WIDTH = 4
LRU_C = 8.0
X_HEADS = 4
N_BUCKETS = 32
MAX_DISTANCE = 128
EPS = 1e-6
NEG = -1e30

TQ = 256
CK = 256
ROWS = NSA_HPG * TQ
VT_ROWS = HEAD_DIM + 16
LOG2E = 1.4426950408889634
MASKV = -(2.0 ** 100)
M_INIT = -1e30
ROW_TILE = 512
FF_CHUNK = 1024
LRU_TILE = 512
VMEM_LIMIT = 56 * 1024 * 1024

F32 = jnp.float32
BF16 = jnp.bfloat16


def _bucket_of_distance(d):
    max_exact = N_BUCKETS // 2
    d = np.maximum(d, 0)
    df = np.maximum(d, 1).astype(np.float64)
    large = max_exact + (np.log(df / max_exact) / math.log(MAX_DISTANCE / max_exact)
                         * (N_BUCKETS - max_exact)).astype(np.int32)
    large = np.minimum(large, N_BUCKETS - 1)
    return np.where(d < max_exact, d, large).astype(np.int32)


def _rms(x, g):
    return x * lax.rsqrt(jnp.mean(x * x, axis=-1, keepdims=True) + EPS) * g


def _dot(a, b):
    return jnp.dot(a, b, preferred_element_type=F32)


def _params(sem):
    return pltpu.CompilerParams(dimension_semantics=sem, vmem_limit_bytes=VMEM_LIMIT)


def _const_spec(shape):
    nd = len(shape)
    return pl.BlockSpec(shape, lambda *_: (0,) * nd)


def _inproj_kernel(x_ref, g_ref, wq_ref, wc_ref, wkv_ref, wg_ref, wx_ref,
                   q_ref, kcr_ref, vcr_ref, kv_ref, gt_ref, xgr_ref):
    h = _rms(x_ref[...], g_ref[...]).astype(BF16)
    q_ref[...] = (_dot(h, wq_ref[...]) * (HEAD_DIM ** -0.5 * LOG2E)).astype(BF16)
    c = _dot(h, wc_ref[...])
    kcr_ref[...] = c[:, :KV_WIDTH]
    vcr_ref[...] = c[:, KV_WIDTH:]
    kv_ref[...] = _dot(h, wkv_ref[...]).astype(BF16)
    gt_ref[...] = _dot(h, wg_ref[...])
    xgr_ref[...] = _dot(h, wx_ref[...])


def _inproj(x2, gain, wq, wc, wkv, wg, wx):
    n, d = x2.shape
    tm = ROW_TILE
    row = lambda w: pl.BlockSpec((tm, w), lambda i: (i, 0))
    return pl.pallas_call(
        _inproj_kernel,
        grid=(n // tm,),
        in_specs=[row(d), _const_spec(gain.shape), _const_spec(wq.shape), _const_spec(wc.shape),
                  _const_spec(wkv.shape), _const_spec(wg.shape), _const_spec(wx.shape)],
        out_specs=[row(NSA_WIDTH), row(KV_WIDTH), row(KV_WIDTH), row(4 * KV_WIDTH), row(128),
                   row(wx.shape[1])],
        out_shape=[jax.ShapeDtypeStruct((n, NSA_WIDTH), BF16),
                   jax.ShapeDtypeStruct((n, KV_WIDTH), F32),
                   jax.ShapeDtypeStruct((n, KV_WIDTH), F32),
                   jax.ShapeDtypeStruct((n, 4 * KV_WIDTH), BF16),
                   jax.ShapeDtypeStruct((n, 128), F32),
                   jax.ShapeDtypeStruct((n, wx.shape[1]), F32)],
        compiler_params=_params(("parallel",)),
        name="inproj",
    )(x2, gain, wq, wc, wkv, wg, wx)


def _compress_kernel(kr_ref, vr_ref, pek_ref, pev_ref, wk_ref, wv_ref, kc_ref, vc_ref):
    kc_ref[...] = _dot((kr_ref[...] + pek_ref[...]).astype(BF16), wk_ref[...]).astype(BF16)
    vc_ref[...] = _dot((vr_ref[...] + pev_ref[...]).astype(BF16), wv_ref[...]).astype(BF16)


def _compress(kr, vr, pek, pev, wk, wv):
    nblk, width = kr.shape
    tm = 128
    row = lambda w: pl.BlockSpec((tm, w), lambda i: (i, 0))
    return pl.pallas_call(
        _compress_kernel,
        grid=(nblk // tm,),
        in_specs=[row(width), row(width), _const_spec(pek.shape), _const_spec(pev.shape),
                  _const_spec(wk.shape), _const_spec(wv.shape)],
        out_specs=[row(KV_WIDTH), row(KV_WIDTH)],
        out_shape=[jax.ShapeDtypeStruct((nblk, KV_WIDTH), BF16)] * 2,
        compiler_params=_params(("parallel",)),
        name="compress",
    )(kr, vr, pek, pev, wk, wv)


def _bias_tables_kernel(rb_ref, bma_ref, bmb_ref, bm1_ref, bm2_ref,
                        ta_ref, tb_ref, d1_ref, d2_ref):
    last = N_BUCKETS - 1
    for hd in range(NSA_HEADS):
        g, hl = divmod(hd, NSA_HPG)
        cols = slice(hl * TQ, (hl + 1) * TQ)
        far = rb_ref[last, hd]
        for bm_ref, out_ref, masked in ((bma_ref, ta_ref, False), (bmb_ref, tb_ref, True),
                                        (bm1_ref, d1_ref, False), (bm2_ref, d2_ref, False)):
            bm = bm_ref[...]
            acc = jnp.zeros(bm.shape, F32)
            for k in range(N_BUCKETS - 1):
                acc = jnp.where(bm == k, (rb_ref[k, hd] - far) * LOG2E, acc)
            if masked:
                acc = jnp.where(bm < 0, MASKV, acc)
            out_ref[g, :, cols] = acc


def _bias_tables(rel_bias):
    ki = np.arange(CK)[:, None]
    qi = np.arange(TQ)[None, :]
    bma = _bucket_of_distance(qi - ki + CK)
    bmb = np.where(qi >= ki, _bucket_of_distance(qi - ki), -1).astype(np.int32)
    r = (np.arange(TQ) + 1) % CMP_BLOCK
    bm1 = np.broadcast_to(_bucket_of_distance(r)[None, :], (8, TQ)).astype(np.int32)
    bm2 = np.broadcast_to(_bucket_of_distance(r + CMP_BLOCK)[None, :], (8, TQ)).astype(np.int32)
    return pl.pallas_call(
        _bias_tables_kernel,
        in_specs=[pl.BlockSpec(memory_space=pltpu.SMEM)] + [pl.BlockSpec(memory_space=pltpu.VMEM)] * 4,
        out_specs=[pl.BlockSpec(memory_space=pltpu.VMEM)] * 4,
        out_shape=[jax.ShapeDtypeStruct((NSA_KV_HEADS, CK, ROWS), F32),
                   jax.ShapeDtypeStruct((NSA_KV_HEADS, CK, ROWS), F32),
                   jax.ShapeDtypeStruct((NSA_KV_HEADS, 8, ROWS), F32),
                   jax.ShapeDtypeStruct((NSA_KV_HEADS, 8, ROWS), F32)],
        name="bias_tables",
    )(rel_bias, jnp.asarray(bma), jnp.asarray(bmb), jnp.asarray(bm1), jnp.asarray(bm2))


def _attend(chains):
    scores = []
    for chunks, _, _ in chains:
        row = []
        for keys, q, bias, _ in chunks:
            s = _dot(keys, q)
            row.append(s if bias is None else s + bias)
        scores.append(row)
    for (chunks, m_ref, acc_ref), row in zip(chains, scores):
        m = m_ref[...]
        acc = acc_ref[...]
        for (_, _, _, vt), s in zip(chunks, row):
            m_new = jnp.maximum(m, jnp.max(s, axis=0, keepdims=True))
            p = jnp.exp2(s - m_new).astype(BF16)
            acc = jnp.exp2(m - m_new) * acc + _dot(vt, p)
            m = m_new
        m_ref[...] = m
        acc_ref[...] = acc


def _attn_kernel(q_ref, gt_ref, kc_ref, vc_ref, ks_ref, vs_ref, kw_ref, vw_ref,
                 ta_ref, tb_ref, d1_ref, d2_ref, out_ref,
                 ka_ref, vst_ref, vwt_ref, vct_ref, qa_ref, oc_ref, ms_ref, accs_ref, mw_ref, accw_ref, outt_ref):
    qt = pl.program_id(1)
    n_blocks = kc_ref.shape[1]
    blocks_per_tile = TQ // CMP_BLOCK

    @pl.when(qt == 0)
    def _():
        ones_rows = jnp.where(lax.broadcasted_iota(jnp.int32, (VT_ROWS - HEAD_DIM, CK), 0) == 0,
                              1.0, 0.0).astype(BF16)

        def build(c, carry):
            rs = pl.ds(pl.multiple_of(c * CK, CK), CK)
            blk = lax.broadcasted_iota(jnp.int32, (CK, 128), 1)
            key_blk = c * (CK // CMP_BLOCK) + (lax.broadcasted_iota(jnp.int32, (CK, 128), 0) >> 6)
            ka_ref[rs, 0:128] = jnp.where(blk == key_blk, 1.0, 0.0).astype(BF16)
            ka_ref[rs, 128:256] = ks_ref[0, rs, :]
            vs_t = vs_ref[0, rs, :].astype(F32).T
            vw_t = vw_ref[0, rs, :].astype(F32).T
            for g in range(NSA_KV_HEADS):
                gs = slice(g * HEAD_DIM, (g + 1) * HEAD_DIM)
                vst_ref[c, g, 0:HEAD_DIM, :] = vs_t[gs].astype(BF16)
                vst_ref[c, g, HEAD_DIM:, :] = ones_rows
                vwt_ref[c, g, 0:HEAD_DIM, :] = vw_t[gs].astype(BF16)
                vwt_ref[c, g, HEAD_DIM:, :] = ones_rows
            return carry
        lax.fori_loop(0, ks_ref.shape[1] // CK, build, 0)
        vct_ref[...] = vc_ref[0].astype(F32).T.astype(BF16)

    q_t = q_ref[0].astype(F32).T
    gates_t = jax.nn.sigmoid(gt_ref[0]).T
    kc = kc_ref[0]

    q_loc = lax.broadcasted_iota(jnp.int32, (1, ROWS), 1) & (TQ - 1)
    n_iota = lax.broadcasted_iota(jnp.int32, (n_blocks, ROWS), 0)
    n1 = qt * blocks_per_tile + ((q_loc + 1) >> 6) - 1
    j_s = lax.broadcasted_iota(jnp.int32, (n_blocks, TQ), 0)
    j_f = j_s.astype(F32)
    cur = qt * blocks_per_tile + (lax.broadcasted_iota(jnp.int32, (1, TQ), 1) >> 6)
    zeros_half = jnp.zeros((HEAD_DIM, ROWS), BF16)

    def chunk_rows(c):
        return pl.ds(pl.multiple_of(c * CK, CK), CK)

    for g in range(NSA_KV_HEADS):
        gs = slice(g * HEAD_DIM, (g + 1) * HEAD_DIM)
        q_g = jnp.concatenate(
            [q_t[(g * NSA_HPG + hl) * HEAD_DIM:(g * NSA_HPG + hl + 1) * HEAD_DIM, :]
             for hl in range(NSA_HPG)], axis=1).astype(BF16)
        q_both = jnp.concatenate([q_g, zeros_half] if g == 0 else [zeros_half, q_g], axis=0)

        s_c = _dot(kc, q_both)
        bias_c = (jnp.where(n_iota == n1, d1_ref[g, 0:1, :], 0.0)
                  + jnp.where(n_iota == n1 - 1, d2_ref[g, 0:1, :], 0.0))
        valid_c = n_iota <= n1
        l_c = jnp.where(valid_c, s_c + bias_c, NEG)
        m_c = jnp.max(l_c, axis=0, keepdims=True)
        e_c = jnp.where(valid_c, jnp.exp2(l_c - m_c), 0.0)
        p_c = e_c / jnp.maximum(jnp.sum(e_c, axis=0, keepdims=True), 1e-30)
        oc_ref[g] = _dot(vct_ref[gs, :], p_c.astype(BF16))

        imp = p_c[:, 0:TQ]
        for hl in range(1, NSA_HPG):
            imp = imp + p_c[:, hl * TQ:(hl + 1) * TQ]
        forced = (j_s == 0) | (j_s == cur) | (j_s == cur - 1)
        work = jnp.where(j_s <= cur, jnp.where(forced, FORCE_SCORE, imp), NEG)
        chosen = jnp.zeros((n_blocks, TQ), F32)
        for _ in range(min(SEL_TOPN, n_blocks)):
            mx = jnp.max(work, axis=0, keepdims=True)
            first = jnp.min(jnp.where(work == mx, j_f, float(n_blocks)), axis=0, keepdims=True)
            pick = j_f == first
            chosen = jnp.where(pick, jnp.where(mx > NEG / 2, 1.0, 0.0), chosen)
            work = jnp.where(pick, -jnp.inf, work)
        sel_mask = jnp.where(chosen > 0.5, 0.0, MASKV).astype(BF16)
        qa_ref[g, 0:128, :] = jnp.concatenate([sel_mask] * NSA_HPG, axis=1)
        qa_ref[g, 128:256, :] = q_both

        for m_ref, acc_ref in ((ms_ref, accs_ref), (mw_ref, accw_ref)):
            m_ref[g] = jnp.full((1, ROWS), M_INIT, F32)
            acc_ref[g] = jnp.zeros((VT_ROWS, ROWS), F32)

    groups = range(NSA_KV_HEADS)

    def sel_chains(cs, biases=None):
        keys = [ka_ref[chunk_rows(c), :] for c in cs]
        biases = biases or [None] * len(cs)
        return [([(k, qa_ref[g], bias if bias is None else bias[g], vst_ref[c, g])
                  for k, c, bias in zip(keys, cs, biases)], ms_ref.at[g], accs_ref.at[g]) for g in groups]

    def win_chains(cs, biases):
        keys = [kw_ref[0, chunk_rows(c), :] for c in cs]
        return [([(k, qa_ref[g, 128:256, :], bias[g], vwt_ref[c, g])
                  for k, c, bias in zip(keys, cs, biases)], mw_ref.at[g], accw_ref.at[g]) for g in groups]

    n_far = jnp.maximum(qt - 1, 0)

    def far4(i, carry):
        _attend(sel_chains([4 * i + k for k in range(4)]))
        return carry
    lax.fori_loop(0, n_far >> 2, far4, 0)
    done4 = (n_far >> 2) << 2

    @pl.when((n_far & 2) != 0)
    def _():
        _attend(sel_chains([done4, done4 + 1]))

    @pl.when((n_far & 1) != 0)
    def _():
        _attend(sel_chains([done4 + (n_far & 2)]))

    @pl.when(qt >= 2)
    def _():
        w2_mask = jnp.where(q_loc < lax.broadcasted_iota(jnp.int32, (CK, ROWS), 0), 0.0, MASKV)
        ta, tb = [ta_ref[g] for g in groups], [tb_ref[g] for g in groups]
        _attend(sel_chains([qt - 1, qt], [ta, tb])
                + win_chains([qt - 2, qt - 1, qt], [[w2_mask] * NSA_KV_HEADS, ta, tb]))

    @pl.when(qt == 1)
    def _():
        ta, tb = [ta_ref[g] for g in groups], [tb_ref[g] for g in groups]
        _attend(sel_chains([0, 1], [ta, tb]) + win_chains([0, 1], [ta, tb]))

    @pl.when(qt == 0)
    def _():
        tb = [tb_ref[g] for g in groups]
        _attend(sel_chains([0], [tb]) + win_chains([0], [tb]))

    for g in range(NSA_KV_HEADS):
        o_s = accs_ref[g, 0:HEAD_DIM, :] / accs_ref[g, HEAD_DIM:HEAD_DIM + 1, :]
        o_w = accw_ref[g, 0:HEAD_DIM, :] / accw_ref[g, HEAD_DIM:HEAD_DIM + 1, :]

        def gate(branch):
            return jnp.concatenate(
                [gates_t[(g * NSA_HPG + hl) * 3 + branch:(g * NSA_HPG + hl) * 3 + branch + 1, :]
                 for hl in range(NSA_HPG)], axis=1)
        mixed = gate(0) * oc_ref[g] + gate(1) * o_s + gate(2) * o_w
        for hl in range(NSA_HPG):
            hd = g * NSA_HPG + hl
            outt_ref[hd * HEAD_DIM:(hd + 1) * HEAD_DIM, :] = mixed[:, hl * TQ:(hl + 1) * TQ]

    out_ref[0] = outt_ref[...].T


def _attention(q, gates, kc, vc, kv, tabs):
    b, s, _ = q.shape
    n_blocks = s // CMP_BLOCK
    nch = s // CK
    ta, tb, d1, d2 = tabs
    tile = lambda w: pl.BlockSpec((1, TQ, w), lambda bi, qi: (bi, qi, 0))
    per_b = lambda r, w, col: pl.BlockSpec((1, r, w), lambda bi, qi: (bi, 0, col))
    return pl.pallas_call(
        _attn_kernel,
        grid=(b, s // TQ),
        in_specs=[tile(NSA_WIDTH), tile(128),
                  per_b(n_blocks, KV_WIDTH, 0), per_b(n_blocks, KV_WIDTH, 0),
                  per_b(s, KV_WIDTH, 0), per_b(s, KV_WIDTH, 1), per_b(s, KV_WIDTH, 2), per_b(s, KV_WIDTH, 3),
                  _const_spec(ta.shape), _const_spec(tb.shape), _const_spec(d1.shape), _const_spec(d2.shape)],
        out_specs=tile(NSA_WIDTH),
        out_shape=jax.ShapeDtypeStruct((b, s, NSA_WIDTH), F32),
        scratch_shapes=[pltpu.VMEM((s, CK), BF16),
                        pltpu.VMEM((nch, NSA_KV_HEADS, VT_ROWS, CK), BF16),
                        pltpu.VMEM((nch, NSA_KV_HEADS, VT_ROWS, CK), BF16),
                        pltpu.VMEM((KV_WIDTH, n_blocks), BF16),
                        pltpu.VMEM((NSA_KV_HEADS, CK, ROWS), BF16),
                        pltpu.VMEM((NSA_KV_HEADS, HEAD_DIM, ROWS), F32),
                        pltpu.VMEM((NSA_KV_HEADS, 1, ROWS), F32), pltpu.VMEM((NSA_KV_HEADS, VT_ROWS, ROWS), F32),
                        pltpu.VMEM((NSA_KV_HEADS, 1, ROWS), F32), pltpu.VMEM((NSA_KV_HEADS, VT_ROWS, ROWS), F32),
                        pltpu.VMEM((NSA_WIDTH, TQ), F32)],
        compiler_params=_params(("parallel", "arbitrary")),
        name="nsa_attention",
    )(q, gates, kc, vc, kv, kv, kv, kv, ta, tb, d1, d2)


def _lru_kernel(xgr_ref, cw_ref, cb_ref, wa_ref, ba_ref, wx_ref, bx_ref, lam_ref,
                out_ref, xbuf_ref, h_ref):
    t = pl.program_id(1)
    ts = LRU_TILE
    w = out_ref.shape[2]

    @pl.when(t == 0)
    def _():
        xbuf_ref[0:8, :] = jnp.zeros((8, w), F32)
        h_ref[...] = jnp.zeros_like(h_ref)

    xg = xgr_ref[0, :, :w]
    xbuf_ref[8:, :] = xgr_ref[0, :, w:]
    xc = cb_ref[...] + xbuf_ref[8:, :] * cw_ref[CONV_WIDTH - 1:CONV_WIDTH, :]
    for k in range(1, CONV_WIDTH):
        xc = xc + xbuf_ref[8 - k:8 - k + ts, :] * cw_ref[CONV_WIDTH - 1 - k:CONV_WIDTH - k, :]
    xbuf_ref[0:8, :] = xbuf_ref[ts:ts + 8, :]

    xcb = xc.astype(BF16)
    r = jax.nn.sigmoid(_dot(xcb, wa_ref[...]) + ba_ref[...])
    i = jax.nn.sigmoid(_dot(xcb, wx_ref[...]) + bx_ref[...])
    nl = -lam_ref[...]
    softplus = jnp.maximum(nl, 0.0) + jnp.log(1.0 + jnp.exp(-jnp.abs(nl)))
    log_a = (-LRU_C * softplus) * r
    a = jnp.exp(log_a)
    bv = jnp.sqrt(1.0 - a * a) * (i * xc)

    row8 = lax.broadcasted_iota(jnp.int32, (ts, w), 0) & 7
    sh = 1
    while sh < 8:
        a_prev = jnp.where(row8 >= sh, pltpu.roll(a, sh, 0), 1.0)
        b_prev = jnp.where(row8 >= sh, pltpu.roll(bv, sh, 0), 0.0)
        bv = a * b_prev + bv
        a = a * a_prev
        sh *= 2
    y = jax.nn.gelu(xg)
    hc = h_ref[0:1, :]
    for grp in range(ts // 8):
        rs = slice(8 * grp, 8 * grp + 8)
        hg = bv[rs] + a[rs] * hc
        out_ref[0, rs, :] = hg * y[rs]
        hc = hg[7:8]
    h_ref[0:1, :] = hc


def _lru(xgr, cw, cb, wa, ba, wx, bx, lam):
    b, s, w2 = xgr.shape
    w = w2 // 2
    ts = LRU_TILE
    return pl.pallas_call(
        _lru_kernel,
        grid=(b, s // ts),
        in_specs=[pl.BlockSpec((1, ts, w2), lambda bi, ti: (bi, ti, 0)),
                  _const_spec(cw.shape), _const_spec(cb.shape), _const_spec(wa.shape), _const_spec(ba.shape),
                  _const_spec(wx.shape), _const_spec(bx.shape), _const_spec(lam.shape)],
        out_specs=pl.BlockSpec((1, ts, w), lambda bi, ti: (bi, ti, 0)),
        out_shape=jax.ShapeDtypeStruct((b, s, w), F32),
        scratch_shapes=[pltpu.VMEM((ts + 8, w), F32), pltpu.VMEM((8, w), F32)],
        compiler_params=_params(("parallel", "arbitrary")),
        name="rg_lru",
    )(xgr, cw, cb, wa, ba, wx, bx, lam)


def _post_kernel(x_ref, att_ref, lru_ref, kt_ref, v_ref,
                 ga_ref, gl_ref, wa_ref, wl_ref, gmix_ref,
                 gxpre_ref, wq_ref, wo_ref, gxpost_ref,
                 gmpre_ref, w1_ref, w2_ref, gmpost_ref, out_ref):
    x = x_ref[0]
    d = x.shape[1]
    a = _rms(att_ref[0], ga_ref[...]).astype(BF16)
    l = _rms(lru_ref[0], gl_ref[...]).astype(BF16)
    x = x + _rms(_dot(a, wa_ref[...]) + _dot(l, wl_ref[...]), gmix_ref[...])

    dh = d // X_HEADS
    h = _rms(x, gxpre_ref[...]).astype(BF16)
    cq = (_dot(h, wq_ref[...]) * (dh ** -0.5)).astype(BF16)
    outs = []
    for hh in range(X_HEADS):
        hs = slice(hh * dh, (hh + 1) * dh)
        s = _dot(cq[:, hs], kt_ref[0, hs, :])
        e = jnp.exp(s - jnp.max(s, axis=1, keepdims=True))
        p = e / jnp.sum(e, axis=1, keepdims=True)
        outs.append(_dot(p.astype(BF16), v_ref[0, :, hs]).astype(BF16))
    x = x + _rms(_dot(jnp.concatenate(outs, axis=1), wo_ref[...]), gxpost_ref[...])

    h = _rms(x, gmpre_ref[...]).astype(BF16)
    acc = jnp.zeros(x.shape, F32)
    for c in range(w1_ref.shape[1] // FF_CHUNK):
        cs = slice(c * FF_CHUNK, (c + 1) * FF_CHUNK)
        u = jnp.maximum(_dot(h, w1_ref[:, cs]), 0.0)
        acc = acc + _dot((u * u).astype(BF16), w2_ref[cs, :])
    out_ref[0] = x + _rms(acc, gmpost_ref[...])


def _post(x, att, lru, kt, v, gains_and_weights):
    b, s, d = x.shape
    m = v.shape[1]
    tm = ROW_TILE
    row = lambda w: pl.BlockSpec((1, tm, w), lambda bi, i: (bi, i, 0))
    resident = lambda a: pl.BlockSpec(a.shape, lambda *_: (0,) * a.ndim, pipeline_mode=pl.Buffered(1))
    return pl.pallas_call(
        _post_kernel,
        grid=(b, s // tm),
        in_specs=[row(d), row(att.shape[2]), row(lru.shape[2]),
                  pl.BlockSpec((1, d, m), lambda bi, i: (bi, 0, 0)),
                  pl.BlockSpec((1, m, d), lambda bi, i: (bi, 0, 0))]
                 + [resident(a) for a in gains_and_weights],
        out_specs=row(d),
        out_shape=jax.ShapeDtypeStruct((b, s, d), F32),
        compiler_params=_params(("parallel", "parallel")),
        name="post_mixer",
    )(x, att, lru, kt, v, *gains_and_weights)


def _memkv_kernel(mem_ref, g_ref, wk_ref, wv_ref, kt_ref, v_ref):
    mn = _rms(mem_ref[0], g_ref[...]).astype(BF16)
    kt_ref[0] = _dot(mn, wk_ref[...]).T.astype(BF16)
    v_ref[0] = _dot(mn, wv_ref[...]).astype(BF16)


def _memkv(mem, g, wk, wv):
    b, m, d = mem.shape
    return pl.pallas_call(
        _memkv_kernel,
        grid=(b,),
        in_specs=[pl.BlockSpec((1, m, d), lambda i: (i, 0, 0)), _const_spec(g.shape),
                  _const_spec(wk.shape), _const_spec(wv.shape)],
        out_specs=[pl.BlockSpec((1, d, m), lambda i: (i, 0, 0)), pl.BlockSpec((1, m, d), lambda i: (i, 0, 0))],
        out_shape=[jax.ShapeDtypeStruct((b, d, m), BF16), jax.ShapeDtypeStruct((b, m, d), BF16)],
        compiler_params=_params(("parallel",)),
        name="mem_kv",
    )(mem, g, wk, wv)


def _block_structured(w):
    eye = jnp.eye(NSA_KV_HEADS, dtype=w.dtype)
    big = w[:, None, :, None, :] * eye[None, :, None, :, None]
    return big.reshape(CMP_BLOCK * KV_WIDTH, KV_WIDTH).astype(BF16)


def _block_diag(w):
    n, d, e = w.shape
    eye = jnp.eye(n, dtype=w.dtype)
    return (w[:, :, None, :] * eye[:, None, :, None]).reshape(n * d, n * e).astype(BF16)


def kernel(x, mem, rel_bias, ln_mix_pre, ln_mix_post, w_in, cmp_pe_k, cmp_pe_v, cmp_w_k, cmp_w_v,
           conv_w, conv_b, lru_wa, lru_ba, lru_wx, lru_bx, lru_lambda, gn_attn, gn_lru, w_out,
           ln_x_pre, ln_x_post, ln_mem, xq, xkv, xo, ln_mlp_pre, ln_mlp_post, mlp_w1, mlp_w2):
    b, s, d = x.shape
    depth = w_in.shape[0]
    n = b * s
    lru_w = conv_w.shape[2]
    assert s % TQ == 0 and s % LRU_TILE == 0 and n % ROW_TILE == 0
    assert s // CMP_BLOCK == 128, "selection mask layout assumes 128 compression blocks"
    row1 = lambda v: v.reshape(1, -1)

    tabs = _bias_tables(rel_bias)
    x2 = x.reshape(n, d)
    o_q = NSA_WIDTH
    o_g = o_q + 6 * KV_WIDTH
    o_x = o_g + 3 * NSA_HEADS
    for l in range(depth):
        wl = w_in[l]
        wq = wl[:, :o_q].astype(BF16)
        wc = wl[:, o_q:o_q + 2 * KV_WIDTH].astype(BF16)
        wkv = wl[:, o_q + 2 * KV_WIDTH:o_g].astype(BF16)
        wg = jnp.pad(wl[:, o_g:o_x], ((0, 0), (0, 128 - 3 * NSA_HEADS))).astype(BF16)
        wx = wl[:, o_x:].astype(BF16)
        q, kcr, vcr, kv, gt, xgr = _inproj(x2, row1(ln_mix_pre[l]), wq, wc, wkv, wg, wx)

        nblk = n // CMP_BLOCK
        pek = jnp.tile(cmp_pe_k[l], (1, NSA_KV_HEADS)).reshape(1, -1)
        pev = jnp.tile(cmp_pe_v[l], (1, NSA_KV_HEADS)).reshape(1, -1)
        kc, vc = _compress(kcr.reshape(nblk, CMP_BLOCK * KV_WIDTH), vcr.reshape(nblk, CMP_BLOCK * KV_WIDTH),
                           pek, pev, _block_structured(cmp_w_k[l]), _block_structured(cmp_w_v[l]))

        att = _attention(q.reshape(b, s, NSA_WIDTH), gt.reshape(b, s, 128),
                         kc.reshape(b, s // CMP_BLOCK, KV_WIDTH), vc.reshape(b, s // CMP_BLOCK, KV_WIDTH),
                         kv.reshape(b, s, 4 * KV_WIDTH), tabs)
        lru = _lru(xgr.reshape(b, s, 2 * lru_w), conv_w[l], row1(conv_b[l]),
                   _block_diag(lru_wa[l]), row1(lru_ba[l]), _block_diag(lru_wx[l]), row1(lru_bx[l]),
                   row1(lru_lambda[l]))

        wo = w_out[l].astype(BF16)
        wkv_x = xkv[l].astype(BF16)
        kt, v = _memkv(mem, row1(ln_mem[l]), wkv_x[:, :d], wkv_x[:, d:])
        x2 = _post(x2.reshape(b, s, d), att, lru, kt, v,
                   (row1(gn_attn[l]), row1(gn_lru[l]), wo[:NSA_WIDTH], wo[NSA_WIDTH:], row1(ln_mix_post[l]),
                    row1(ln_x_pre[l]), xq[l].astype(BF16), xo[l].astype(BF16), row1(ln_x_post[l]),
                    row1(ln_mlp_pre[l]), mlp_w1[l].astype(BF16), mlp_w2[l].astype(BF16),
                    row1(ln_mlp_post[l]))).reshape(n, d)
    return x2.reshape(b, s, d)
```

```python
import functools
import math

import numpy as np
import jax
import jax.numpy as jnp
from jax import lax
from jax.experimental import pallas as pl
from jax.experimental.pallas import tpu as pltpu

HEAD_DIM = 64
NSA_HEADS = 8
NSA_KV_HEADS = 2
NSA_HPG = NSA_HEADS // NSA_KV_HEADS
NSA_WIDTH = NSA_HEADS * HEAD_DIM
KV_WIDTH = NSA_KV_HEADS * HEAD_DIM
CMP_BLOCK = 64
SEL_TOPN = 16
WINDOW = 512
FORCE_SCORE = 1e4
LRU_BLOCKS = 8
CONV_WIDTH = 4
LRU_C = 8.0
X_HEADS = 4
N_BUCKETS = 32
MAX_DISTANCE = 128
EPS = 1e-6
NEG = -1e30

TQ = 256
CK = 256
ROWS = NSA_HPG * TQ
VT_ROWS = HEAD_DIM + 16
LOG2E = 1.4426950408889634
PIPE_DEPTH = 4
MASKV = -(2.0 ** 100)
M_INIT = -1e30
MAX_RISE = 64.0
ROW_TILE = 512
FF_CHUNK = 1024
LRU_TILE = 512
VMEM_LIMIT = 56 * 1024 * 1024

F32 = jnp.float32
BF16 = jnp.bfloat16


def _bucket_of_distance(d):
    max_exact = N_BUCKETS // 2
    d = np.maximum(d, 0)
    df = np.maximum(d, 1).astype(np.float64)
    large = max_exact + (np.log(df / max_exact) / math.log(MAX_DISTANCE / max_exact)
                         * (N_BUCKETS - max_exact)).astype(np.int32)
    large = np.minimum(large, N_BUCKETS - 1)
    return np.where(d < max_exact, d, large).astype(np.int32)


def _rms(x, g):
    return x * lax.rsqrt(jnp.mean(x * x, axis=-1, keepdims=True) + EPS) * g


def _dot(a, b):
    return jnp.dot(a, b, preferred_element_type=F32)


def _params(sem):
    return pltpu.CompilerParams(dimension_semantics=sem, vmem_limit_bytes=VMEM_LIMIT)


def _const_spec(shape):
    nd = len(shape)
    return pl.BlockSpec(shape, lambda *_: (0,) * nd)


def _inproj_kernel(x_ref, g_ref, wq_ref, wc_ref, wkv_ref, wg_ref, wx_ref,
                   q_ref, kcr_ref, vcr_ref, kv_ref, gt_ref, xgr_ref):
    h = _rms(x_ref[...], g_ref[...]).astype(BF16)
    q_ref[...] = (_dot(h, wq_ref[...]) * (HEAD_DIM ** -0.5 * LOG2E)).astype(BF16)
    c = _dot(h, wc_ref[...])
    kcr_ref[...] = c[:, :KV_WIDTH]
    vcr_ref[...] = c[:, KV_WIDTH:]
    kv_ref[...] = _dot(h, wkv_ref[...]).astype(BF16)
    gt_ref[...] = _dot(h, wg_ref[...])
    xgr_ref[...] = _dot(h, wx_ref[...])


def _inproj(x2, gain, wq, wc, wkv, wg, wx):
    n, d = x2.shape
    tm = ROW_TILE
    row = lambda w: pl.BlockSpec((tm, w), lambda i: (i, 0))
    return pl.pallas_call(
        _inproj_kernel,
        grid=(n // tm,),
        in_specs=[row(d), _const_spec(gain.shape), _const_spec(wq.shape), _const_spec(wc.shape),
                  _const_spec(wkv.shape), _const_spec(wg.shape), _const_spec(wx.shape)],
        out_specs=[row(NSA_WIDTH), row(KV_WIDTH), row(KV_WIDTH), row(4 * KV_WIDTH), row(128),
                   row(wx.shape[1])],
        out_shape=[jax.ShapeDtypeStruct((n, NSA_WIDTH), BF16),
                   jax.ShapeDtypeStruct((n, KV_WIDTH), F32),
                   jax.ShapeDtypeStruct((n, KV_WIDTH), F32),
                   jax.ShapeDtypeStruct((n, 4 * KV_WIDTH), BF16),
                   jax.ShapeDtypeStruct((n, 128), F32),
                   jax.ShapeDtypeStruct((n, wx.shape[1]), F32)],
        compiler_params=_params(("parallel",)),
        name="inproj",
    )(x2, gain, wq, wc, wkv, wg, wx)


def _compress_kernel(kr_ref, vr_ref, pek_ref, pev_ref, wk_ref, wv_ref, kc_ref, vc_ref):
    kc_ref[...] = _dot((kr_ref[...] + pek_ref[...]).astype(BF16), wk_ref[...]).astype(BF16)
    vc_ref[...] = _dot((vr_ref[...] + pev_ref[...]).astype(BF16), wv_ref[...]).astype(BF16)


def _compress(kr, vr, pek, pev, wk, wv):
    nblk, width = kr.shape
    tm = 128
    row = lambda w: pl.BlockSpec((tm, w), lambda i: (i, 0))
    return pl.pallas_call(
        _compress_kernel,
        grid=(nblk // tm,),
        in_specs=[row(width), row(width), _const_spec(pek.shape), _const_spec(pev.shape),
                  _const_spec(wk.shape), _const_spec(wv.shape)],
        out_specs=[row(KV_WIDTH), row(KV_WIDTH)],
        out_shape=[jax.ShapeDtypeStruct((nblk, KV_WIDTH), BF16)] * 2,
        compiler_params=_params(("parallel",)),
        name="compress",
    )(kr, vr, pek, pev, wk, wv)


def _bias_tables_kernel(rb_ref, bma_ref, bmb_ref, bm1_ref, bm2_ref,
                        ta_ref, tb_ref, d1_ref, d2_ref):
    last = N_BUCKETS - 1
    for hd in range(NSA_HEADS):
        g, hl = divmod(hd, NSA_HPG)
        cols = slice(hl * TQ, (hl + 1) * TQ)
        far = rb_ref[last, hd]
        for bm_ref, out_ref, masked in ((bma_ref, ta_ref, False), (bmb_ref, tb_ref, True),
                                        (bm1_ref, d1_ref, False), (bm2_ref, d2_ref, False)):
            bm = bm_ref[...]
            acc = jnp.zeros(bm.shape, F32)
            for k in range(N_BUCKETS - 1):
                acc = jnp.where(bm == k, (rb_ref[k, hd] - far) * LOG2E, acc)
            if masked:
                acc = jnp.where(bm < 0, MASKV, acc)
            out_ref[g, :, cols] = acc


def _bias_tables(rel_bias):
    ki = np.arange(CK)[:, None]
    qi = np.arange(TQ)[None, :]
    bma = _bucket_of_distance(qi - ki + CK)
    bmb = np.where(qi >= ki, _bucket_of_distance(qi - ki), -1).astype(np.int32)
    r = (np.arange(TQ) + 1) % CMP_BLOCK
    bm1 = np.broadcast_to(_bucket_of_distance(r)[None, :], (8, TQ)).astype(np.int32)
    bm2 = np.broadcast_to(_bucket_of_distance(r + CMP_BLOCK)[None, :], (8, TQ)).astype(np.int32)
    return pl.pallas_call(
        _bias_tables_kernel,
        in_specs=[pl.BlockSpec(memory_space=pltpu.SMEM)] + [pl.BlockSpec(memory_space=pltpu.VMEM)] * 4,
        out_specs=[pl.BlockSpec(memory_space=pltpu.VMEM)] * 4,
        out_shape=[jax.ShapeDtypeStruct((NSA_KV_HEADS, CK, ROWS), F32),
                   jax.ShapeDtypeStruct((NSA_KV_HEADS, CK, ROWS), F32),
                   jax.ShapeDtypeStruct((NSA_KV_HEADS, 8, ROWS), F32),
                   jax.ShapeDtypeStruct((NSA_KV_HEADS, 8, ROWS), F32)],
        name="bias_tables",
    )(rel_bias, jnp.asarray(bma), jnp.asarray(bmb), jnp.asarray(bm1), jnp.asarray(bm2))


def _attend_pass(chains, fixed_reference):
    items = [(ci, chunk) for k in range(max(len(c[0]) for c in chains))
             for ci, c in enumerate(chains) for chunk in c[0][k:k + 1]]

    def score(item):
        keys, q, bias, _ = item[1]
        s = _dot(keys, q)
        return s if bias is None else s + bias

    scores = [score(item) for item in items[:PIPE_DEPTH]]
    start = [(m_ref[...], acc_ref[...]) for _, m_ref, acc_ref in chains]
    state = list(start)
    tops = [None] * len(chains)
    for i, (ci, chunk) in enumerate(items):
        s = scores[i]
        m, acc = state[ci]
        top = jnp.max(s, axis=0, keepdims=True)
        if fixed_reference:
            tops[ci] = top if tops[ci] is None else jnp.maximum(tops[ci], top)
            state[ci] = (m, acc + _dot(chunk[3], jnp.exp2(s - m).astype(BF16)))
        else:
            m_new = jnp.maximum(m, top)
            p = jnp.exp2(s - m_new).astype(BF16)
            state[ci] = (m_new, jnp.exp2(m - m_new) * acc + _dot(chunk[3], p))
        if i + PIPE_DEPTH < len(items):
            scores.append(score(items[i + PIPE_DEPTH]))
    if not fixed_reference:
        return state, None
    rise = [top - m for top, (m, _) in zip(tops, start)]
    moved = []
    for (m, acc), top in zip(state, tops):
        m_new = jnp.maximum(m, top)
        moved.append((m_new, jnp.exp2(m - m_new) * acc))
    return moved, rise


def _commit(chains, state):
    for (_, m_ref, acc_ref), (m, acc) in zip(chains, state):
        m_ref[...] = m
        acc_ref[...] = acc


def _attend(chains):
    _commit(chains, _attend_pass(chains, fixed_reference=False)[0])


def _attend_from_reference(chains):
    state, rise = _attend_pass(chains, fixed_reference=True)
    worst = rise[0]
    for r in rise[1:]:
        worst = jnp.maximum(worst, r)
    safe = jnp.max(worst) <= MAX_RISE

    @pl.when(safe)
    def _():
        _commit(chains, state)

    @pl.when(jnp.logical_not(safe))
    def _():
        _attend(chains)


def _attn_kernel(q_ref, gt_ref, kc_ref, vc_ref, ks_ref, vs_ref, kw_ref, vw_ref,
                 ta_ref, tb_ref, d1_ref, d2_ref, out_ref,
                 ka_ref, vst_ref, vwt_ref, vct_ref, qa_ref, oc_ref, ms_ref, accs_ref, mw_ref, accw_ref, outt_ref):
    qt = pl.program_id(1)
    n_blocks = kc_ref.shape[1]
    blocks_per_tile = TQ // CMP_BLOCK

    @pl.when(qt == 0)
    def _():
        ones_rows = jnp.where(lax.broadcasted_iota(jnp.int32, (VT_ROWS - HEAD_DIM, CK), 0) == 0,
                              1.0, 0.0).astype(BF16)

        def build(c, carry):
            rs = pl.ds(pl.multiple_of(c * CK, CK), CK)
            blk = lax.broadcasted_iota(jnp.int32, (CK, 128), 1)
            key_blk = c * (CK // CMP_BLOCK) + (lax.broadcasted_iota(jnp.int32, (CK, 128), 0) >> 6)
            ka_ref[rs, 0:128] = jnp.where(blk == key_blk, 1.0, 0.0).astype(BF16)
            ka_ref[rs, 128:256] = ks_ref[0, rs, :]
            vs_t = vs_ref[0, rs, :].astype(F32).T
            vw_t = vw_ref[0, rs, :].astype(F32).T
            for g in range(NSA_KV_HEADS):
                gs = slice(g * HEAD_DIM, (g + 1) * HEAD_DIM)
                vst_ref[c, g, 0:HEAD_DIM, :] = vs_t[gs].astype(BF16)
                vst_ref[c, g, HEAD_DIM:, :] = ones_rows
                vwt_ref[c, g, 0:HEAD_DIM, :] = vw_t[gs].astype(BF16)
                vwt_ref[c, g, HEAD_DIM:, :] = ones_rows
            return carry
        lax.fori_loop(0, ks_ref.shape[1] // CK, build, 0)
        vct_ref[...] = vc_ref[0].astype(F32).T.astype(BF16)

    q_t = q_ref[0].astype(F32).T
    gates_t = jax.nn.sigmoid(gt_ref[0]).T
    kc = kc_ref[0]

    q_loc = lax.broadcasted_iota(jnp.int32, (1, ROWS), 1) & (TQ - 1)
    n_iota = lax.broadcasted_iota(jnp.int32, (n_blocks, ROWS), 0)
    n1 = qt * blocks_per_tile + ((q_loc + 1) >> 6) - 1
    j_s = lax.broadcasted_iota(jnp.int32, (n_blocks, TQ), 0)
    j_f = j_s.astype(F32)
    cur = qt * blocks_per_tile + (lax.broadcasted_iota(jnp.int32, (1, TQ), 1) >> 6)
    zeros_half = jnp.zeros((HEAD_DIM, ROWS), BF16)

    def chunk_rows(c):
        return pl.ds(pl.multiple_of(c * CK, CK), CK)

    for g in range(NSA_KV_HEADS):
        gs = slice(g * HEAD_DIM, (g + 1) * HEAD_DIM)
        q_g = jnp.concatenate(
            [q_t[(g * NSA_HPG + hl) * HEAD_DIM:(g * NSA_HPG + hl + 1) * HEAD_DIM, :]
             for hl in range(NSA_HPG)], axis=1).astype(BF16)
        q_both = jnp.concatenate([q_g, zeros_half] if g == 0 else [zeros_half, q_g], axis=0)

        s_c = _dot(kc, q_both)
        bias_c = (jnp.where(n_iota == n1, d1_ref[g, 0:1, :], 0.0)
                  + jnp.where(n_iota == n1 - 1, d2_ref[g, 0:1, :], 0.0))
        valid_c = n_iota <= n1
        l_c = jnp.where(valid_c, s_c + bias_c, NEG)
        m_c = jnp.max(l_c, axis=0, keepdims=True)
        e_c = jnp.where(valid_c, jnp.exp2(l_c - m_c), 0.0)
        p_c = e_c / jnp.maximum(jnp.sum(e_c, axis=0, keepdims=True), 1e-30)
        oc_ref[g] = _dot(vct_ref[gs, :], p_c.astype(BF16))

        imp = p_c[:, 0:TQ]
        for hl in range(1, NSA_HPG):
            imp = imp + p_c[:, hl * TQ:(hl + 1) * TQ]
        forced = (j_s == 0) | (j_s == cur) | (j_s == cur - 1)
        candidates = jnp.where(forced, NEG, jnp.where(j_s <= cur, imp, NEG))
        work = candidates
        for _ in range(SEL_TOPN - 3):
            mx = jnp.max(work, axis=0, keepdims=True)
            first = jnp.min(jnp.where(work == mx, j_f, float(n_blocks)), axis=0, keepdims=True)
            work = jnp.where(j_f == first, -jnp.inf, work)
        picked_mask = jnp.where(candidates > NEG / 2, jnp.where(work == -jnp.inf, 0.0, MASKV), MASKV)
        sel_mask = jnp.where(forced, 0.0, picked_mask).astype(BF16)
        qa_ref[g, 0:128, :] = jnp.concatenate([sel_mask] * NSA_HPG, axis=1)
        qa_ref[g, 128:256, :] = q_both

        for m_ref, acc_ref in ((ms_ref, accs_ref), (mw_ref, accw_ref)):
            m_ref[g] = jnp.full((1, ROWS), M_INIT, F32)
            acc_ref[g] = jnp.zeros((VT_ROWS, ROWS), F32)

    groups = range(NSA_KV_HEADS)

    def sel_chains(cs, biases=None):
        keys = [ka_ref[chunk_rows(c), :] for c in cs]
        biases = biases or [None] * len(cs)
        return [([(k, qa_ref[g], bias if bias is None else bias[g], vst_ref[c, g])
                  for k, c, bias in zip(keys, cs, biases)], ms_ref.at[g], accs_ref.at[g]) for g in groups]

    def win_chains(cs, biases):
        keys = [kw_ref[0, chunk_rows(c), :] for c in cs]
        return [([(k, qa_ref[g, 128:256, :], bias[g], vwt_ref[c, g])
                  for k, c, bias in zip(keys, cs, biases)], mw_ref.at[g], accw_ref.at[g]) for g in groups]

    @pl.when(qt >= 2)
    def _():
        w2_mask = jnp.where(q_loc < lax.broadcasted_iota(jnp.int32, (CK, ROWS), 0), 0.0, MASKV)
        ta, tb = [ta_ref[g] for g in groups], [tb_ref[g] for g in groups]
        _attend(sel_chains([qt - 1, qt], [ta, tb])
                + win_chains([qt - 2, qt - 1, qt], [[w2_mask] * NSA_KV_HEADS, ta, tb]))

    @pl.when(qt == 1)
    def _():
        ta, tb = [ta_ref[g] for g in groups], [tb_ref[g] for g in groups]
        _attend(sel_chains([0, 1], [ta, tb]) + win_chains([0, 1], [ta, tb]))

    @pl.when(qt == 0)
    def _():
        tb = [tb_ref[g] for g in groups]
        _attend(sel_chains([0], [tb]) + win_chains([0], [tb]))

    n_far = jnp.maximum(qt - 1, 0)

    def far4(i, carry):
        _attend_from_reference(sel_chains([4 * i + k for k in range(4)]))
        return carry
    lax.fori_loop(0, n_far >> 2, far4, 0)
    done4 = (n_far >> 2) << 2

    @pl.when((n_far & 2) != 0)
    def _():
        _attend_from_reference(sel_chains([done4, done4 + 1]))

    @pl.when((n_far & 1) != 0)
    def _():
        _attend_from_reference(sel_chains([done4 + (n_far & 2)]))

    for g in range(NSA_KV_HEADS):
        o_s = accs_ref[g, 0:HEAD_DIM, :] / accs_ref[g, HEAD_DIM:HEAD_DIM + 1, :]
        o_w = accw_ref[g, 0:HEAD_DIM, :] / accw_ref[g, HEAD_DIM:HEAD_DIM + 1, :]

        def gate(branch):
            return jnp.concatenate(
                [gates_t[(g * NSA_HPG + hl) * 3 + branch:(g * NSA_HPG + hl) * 3 + branch + 1, :]
                 for hl in range(NSA_HPG)], axis=1)
        mixed = gate(0) * oc_ref[g] + gate(1) * o_s + gate(2) * o_w
        for hl in range(NSA_HPG):
            hd = g * NSA_HPG + hl
            outt_ref[hd * HEAD_DIM:(hd + 1) * HEAD_DIM, :] = mixed[:, hl * TQ:(hl + 1) * TQ]

    out_ref[0] = outt_ref[...].T


def _attention(q, gates, kc, vc, kv, tabs):
    b, s, _ = q.shape
    n_blocks = s // CMP_BLOCK
    nch = s // CK
    ta, tb, d1, d2 = tabs
    tile = lambda w: pl.BlockSpec((1, TQ, w), lambda bi, qi: (bi, qi, 0))
    per_b = lambda r, w, col: pl.BlockSpec((1, r, w), lambda bi, qi: (bi, 0, col))
    return pl.pallas_call(
        _attn_kernel,
        grid=(b, s // TQ),
        in_specs=[tile(NSA_WIDTH), tile(128),
                  per_b(n_blocks, KV_WIDTH, 0), per_b(n_blocks, KV_WIDTH, 0),
                  per_b(s, KV_WIDTH, 0), per_b(s, KV_WIDTH, 1), per_b(s, KV_WIDTH, 2), per_b(s, KV_WIDTH, 3),
                  _const_spec(ta.shape), _const_spec(tb.shape), _const_spec(d1.shape), _const_spec(d2.shape)],
        out_specs=tile(NSA_WIDTH),
        out_shape=jax.ShapeDtypeStruct((b, s, NSA_WIDTH), F32),
        scratch_shapes=[pltpu.VMEM((s, CK), BF16),
                        pltpu.VMEM((nch, NSA_KV_HEADS, VT_ROWS, CK), BF16),
                        pltpu.VMEM((nch, NSA_KV_HEADS, VT_ROWS, CK), BF16),
                        pltpu.VMEM((KV_WIDTH, n_blocks), BF16),
                        pltpu.VMEM((NSA_KV_HEADS, CK, ROWS), BF16),
                        pltpu.VMEM((NSA_KV_HEADS, HEAD_DIM, ROWS), F32),
                        pltpu.VMEM((NSA_KV_HEADS, 1, ROWS), F32), pltpu.VMEM((NSA_KV_HEADS, VT_ROWS, ROWS), F32),
                        pltpu.VMEM((NSA_KV_HEADS, 1, ROWS), F32), pltpu.VMEM((NSA_KV_HEADS, VT_ROWS, ROWS), F32),
                        pltpu.VMEM((NSA_WIDTH, TQ), F32)],
        compiler_params=_params(("parallel", "arbitrary")),
        name="nsa_attention",
    )(q, gates, kc, vc, kv, kv, kv, kv, ta, tb, d1, d2)


def _lru_kernel(xgr_ref, cw_ref, cb_ref, wa_ref, ba_ref, wx_ref, bx_ref, lam_ref,
                out_ref, xbuf_ref, h_ref):
    t = pl.program_id(1)
    ts = LRU_TILE
    w = out_ref.shape[2]

    @pl.when(t == 0)
    def _():
        xbuf_ref[0:8, :] = jnp.zeros((8, w), F32)
        h_ref[...] = jnp.zeros_like(h_ref)

    xg = xgr_ref[0, :, :w]
    xbuf_ref[8:, :] = xgr_ref[0, :, w:]
    xc = cb_ref[...] + xbuf_ref[8:, :] * cw_ref[CONV_WIDTH - 1:CONV_WIDTH, :]
    for k in range(1, CONV_WIDTH):
        xc = xc + xbuf_ref[8 - k:8 - k + ts, :] * cw_ref[CONV_WIDTH - 1 - k:CONV_WIDTH - k, :]
    xbuf_ref[0:8, :] = xbuf_ref[ts:ts + 8, :]

    xcb = xc.astype(BF16)
    r = jax.nn.sigmoid(_dot(xcb, wa_ref[...]) + ba_ref[...])
    i = jax.nn.sigmoid(_dot(xcb, wx_ref[...]) + bx_ref[...])
    nl = -lam_ref[...]
    softplus = jnp.maximum(nl, 0.0) + jnp.log(1.0 + jnp.exp(-jnp.abs(nl)))
    log_a = (-LRU_C * softplus) * r
    a = jnp.exp(log_a)
    bv = jnp.sqrt(1.0 - a * a) * (i * xc)

    row8 = lax.broadcasted_iota(jnp.int32, (ts, w), 0) & 7
    sh = 1
    while sh < 8:
        a_prev = jnp.where(row8 >= sh, pltpu.roll(a, sh, 0), 1.0)
        b_prev = jnp.where(row8 >= sh, pltpu.roll(bv, sh, 0), 0.0)
        bv = a * b_prev + bv
        a = a * a_prev
        sh *= 2
    y = jax.nn.gelu(xg)
    hc = h_ref[0:1, :]
    for grp in range(ts // 8):
        rs = slice(8 * grp, 8 * grp + 8)
        hg = bv[rs] + a[rs] * hc
        out_ref[0, rs, :] = hg * y[rs]
        hc = hg[7:8]
    h_ref[0:1, :] = hc


def _lru(xgr, cw, cb, wa, ba, wx, bx, lam):
    b, s, w2 = xgr.shape
    w = w2 // 2
    ts = LRU_TILE
    return pl.pallas_call(
        _lru_kernel,
        grid=(b, s // ts),
        in_specs=[pl.BlockSpec((1, ts, w2), lambda bi, ti: (bi, ti, 0)),
                  _const_spec(cw.shape), _const_spec(cb.shape), _const_spec(wa.shape), _const_spec(ba.shape),
                  _const_spec(wx.shape), _const_spec(bx.shape), _const_spec(lam.shape)],
        out_specs=pl.BlockSpec((1, ts, w), lambda bi, ti: (bi, ti, 0)),
        out_shape=jax.ShapeDtypeStruct((b, s, w), F32),
        scratch_shapes=[pltpu.VMEM((ts + 8, w), F32), pltpu.VMEM((8, w), F32)],
        compiler_params=_params(("parallel", "arbitrary")),
        name="rg_lru",
    )(xgr, cw, cb, wa, ba, wx, bx, lam)


def _post_kernel(x_ref, att_ref, lru_ref, kt_ref, v_ref,
                 ga_ref, gl_ref, wa_ref, wl_ref, gmix_ref,
                 gxpre_ref, wq_ref, wo_ref, gxpost_ref,
                 gmpre_ref, w1_ref, w2_ref, gmpost_ref, out_ref):
    x = x_ref[0]
    d = x.shape[1]
    a = _rms(att_ref[0], ga_ref[...]).astype(BF16)
    l = _rms(lru_ref[0], gl_ref[...]).astype(BF16)
    x = x + _rms(_dot(a, wa_ref[...]) + _dot(l, wl_ref[...]), gmix_ref[...])

    dh = d // X_HEADS
    h = _rms(x, gxpre_ref[...]).astype(BF16)
    cq = (_dot(h, wq_ref[...]) * (dh ** -0.5)).astype(BF16)
    outs = []
    for hh in range(X_HEADS):
        hs = slice(hh * dh, (hh + 1) * dh)
        s = _dot(cq[:, hs], kt_ref[0, hs, :])
        e = jnp.exp(s - jnp.max(s, axis=1, keepdims=True))
        p = e / jnp.sum(e, axis=1, keepdims=True)
        outs.append(_dot(p.astype(BF16), v_ref[0, :, hs]).astype(BF16))
    x = x + _rms(_dot(jnp.concatenate(outs, axis=1), wo_ref[...]), gxpost_ref[...])

    h = _rms(x, gmpre_ref[...]).astype(BF16)
    acc = jnp.zeros(x.shape, F32)
    for c in range(w1_ref.shape[1] // FF_CHUNK):
        cs = slice(c * FF_CHUNK, (c + 1) * FF_CHUNK)
        u = jnp.maximum(_dot(h, w1_ref[:, cs]), 0.0)
        acc = acc + _dot((u * u).astype(BF16), w2_ref[cs, :])
    out_ref[0] = x + _rms(acc, gmpost_ref[...])


def _post(x, att, lru, kt, v, gains_and_weights):
    b, s, d = x.shape
    m = v.shape[1]
    tm = ROW_TILE
    row = lambda w: pl.BlockSpec((1, tm, w), lambda bi, i: (bi, i, 0))
    resident = lambda a: pl.BlockSpec(a.shape, lambda *_: (0,) * a.ndim, pipeline_mode=pl.Buffered(1))
    return pl.pallas_call(
        _post_kernel,
        grid=(b, s // tm),
        in_specs=[row(d), row(att.shape[2]), row(lru.shape[2]),
                  pl.BlockSpec((1, d, m), lambda bi, i: (bi, 0, 0)),
                  pl.BlockSpec((1, m, d), lambda bi, i: (bi, 0, 0))]
                 + [resident(a) for a in gains_and_weights],
        out_specs=row(d),
        out_shape=jax.ShapeDtypeStruct((b, s, d), F32),
        compiler_params=_params(("parallel", "parallel")),
        name="post_mixer",
    )(x, att, lru, kt, v, *gains_and_weights)


def _memkv_kernel(mem_ref, g_ref, wk_ref, wv_ref, kt_ref, v_ref):
    mn = _rms(mem_ref[0], g_ref[...]).astype(BF16)
    kt_ref[0] = _dot(mn, wk_ref[...]).T.astype(BF16)
    v_ref[0] = _dot(mn, wv_ref[...]).astype(BF16)


def _memkv(mem, g, wk, wv):
    b, m, d = mem.shape
    return pl.pallas_call(
        _memkv_kernel,
        grid=(b,),
        in_specs=[pl.BlockSpec((1, m, d), lambda i: (i, 0, 0)), _const_spec(g.shape),
                  _const_spec(wk.shape), _const_spec(wv.shape)],
        out_specs=[pl.BlockSpec((1, d, m), lambda i: (i, 0, 0)), pl.BlockSpec((1, m, d), lambda i: (i, 0, 0))],
        out_shape=[jax.ShapeDtypeStruct((b, d, m), BF16), jax.ShapeDtypeStruct((b, m, d), BF16)],
        compiler_params=_params(("parallel",)),
        name="mem_kv",
    )(mem, g, wk, wv)


def _block_structured(w):
    eye = jnp.eye(NSA_KV_HEADS, dtype=w.dtype)
    big = w[:, None, :, None, :] * eye[None, :, None, :, None]
    return big.reshape(CMP_BLOCK * KV_WIDTH, KV_WIDTH).astype(BF16)


def _block_diag(w):
    n, d, e = w.shape
    eye = jnp.eye(n, dtype=w.dtype)
    return (w[:, :, None, :] * eye[:, None, :, None]).reshape(n * d, n * e).astype(BF16)


def kernel(x, mem, rel_bias, ln_mix_pre, ln_mix_post, w_in, cmp_pe_k, cmp_pe_v, cmp_w_k, cmp_w_v,
           conv_w, conv_b, lru_wa, lru_ba, lru_wx, lru_bx, lru_lambda, gn_attn, gn_lru, w_out,
           ln_x_pre, ln_x_post, ln_mem, xq, xkv, xo, ln_mlp_pre, ln_mlp_post, mlp_w1, mlp_w2):
    b, s, d = x.shape
    depth = w_in.shape[0]
    n = b * s
    lru_w = conv_w.shape[2]
    assert s % TQ == 0 and s % LRU_TILE == 0 and n % ROW_TILE == 0
    assert s // CMP_BLOCK == 128, "selection mask layout assumes 128 compression blocks"
    row1 = lambda v: v.reshape(1, -1)

    tabs = _bias_tables(rel_bias)
    x2 = x.reshape(n, d)
    o_q = NSA_WIDTH
    o_g = o_q + 6 * KV_WIDTH
    o_x = o_g + 3 * NSA_HEADS
    for l in range(depth):
        wl = w_in[l]
        wq = wl[:, :o_q].astype(BF16)
        wc = wl[:, o_q:o_q + 2 * KV_WIDTH].astype(BF16)
        wkv = wl[:, o_q + 2 * KV_WIDTH:o_g].astype(BF16)
        wg = jnp.pad(wl[:, o_g:o_x], ((0, 0), (0, 128 - 3 * NSA_HEADS))).astype(BF16)
        wx = wl[:, o_x:].astype(BF16)
        q, kcr, vcr, kv, gt, xgr = _inproj(x2, row1(ln_mix_pre[l]), wq, wc, wkv, wg, wx)

        nblk = n // CMP_BLOCK
        pek = jnp.tile(cmp_pe_k[l], (1, NSA_KV_HEADS)).reshape(1, -1)
        pev = jnp.tile(cmp_pe_v[l], (1, NSA_KV_HEADS)).reshape(1, -1)
        kc, vc = _compress(kcr.reshape(nblk, CMP_BLOCK * KV_WIDTH), vcr.reshape(nblk, CMP_BLOCK * KV_WIDTH),
                           pek, pev, _block_structured(cmp_w_k[l]), _block_structured(cmp_w_v[l]))

        att = _attention(q.reshape(b, s, NSA_WIDTH), gt.reshape(b, s, 128),
                         kc.reshape(b, s // CMP_BLOCK, KV_WIDTH), vc.reshape(b, s // CMP_BLOCK, KV_WIDTH),
                         kv.reshape(b, s, 4 * KV_WIDTH), tabs)
        lru = _lru(xgr.reshape(b, s, 2 * lru_w), conv_w[l], row1(conv_b[l]),
                   _block_diag(lru_wa[l]), row1(lru_ba[l]), _block_diag(lru_wx[l]), row1(lru_bx[l]),
                   row1(lru_lambda[l]))

        wo = w_out[l].astype(BF16)
        wkv_x = xkv[l].astype(BF16)
        kt, v = _memkv(mem, row1(ln_mem[l]), wkv_x[:, :d], wkv_x[:, d:])
        x2 = _post(x2.reshape(b, s, d), att, lru, kt, v,
                   (row1(gn_attn[l]), row1(gn_lru[l]), wo[:NSA_WIDTH], wo[NSA_WIDTH:], row1(ln_mix_post[l]),
                    row1(ln_x_pre[l]), xq[l].astype(BF16), xo[l].astype(BF16), row1(ln_x_post[l]),
                    row1(ln_mlp_pre[l]), mlp_w1[l].astype(BF16), mlp_w2[l].astype(BF16),
                    row1(ln_mlp_post[l]))).reshape(n, d)
    return x2.reshape(b, s, d)
```

```python
import functools
import math

import numpy as np
import jax
import jax.numpy as jnp
from jax import lax
from jax.experimental import pallas as pl
from jax.experimental.pallas import tpu as pltpu

HEAD_DIM = 64
NSA_HEADS = 8
NSA_KV_HEADS = 2
NSA_HPG = NSA_HEADS // NSA_KV_HEADS
NSA_WIDTH = NSA_HEADS * HEAD_DIM
KV_WIDTH = NSA_KV_HEADS * HEAD_DIM
CMP_BLOCK = 64
SEL_TOPN = 16
WINDOW = 512
FORCE_SCORE = 1e4
LRU_BLOCKS = 8
CONV_WIDTH = 4
LRU_C = 8.0
X_HEADS = 4
N_BUCKETS = 32
MAX_DISTANCE = 128
EPS = 1e-6
NEG = -1e30

TQ = 256
CK = 256
ROWS = NSA_HPG * TQ
VT_ROWS = HEAD_DIM + 16
LOG2E = 1.4426950408889634
FAR_BLOCK = 8
PIPE_DEPTH = 4
MASKV = -(2.0 ** 100)
M_INIT = -1e30
MAX_RISE = 64.0
ROW_TILE = 512
FF_CHUNK = 1024
ROW_SPLIT = 2
LRU_TILE = 512
VMEM_LIMIT = 56 * 1024 * 1024

F32 = jnp.float32
BF16 = jnp.bfloat16


def _bucket_of_distance(d):
    max_exact = N_BUCKETS // 2
    d = np.maximum(d, 0)
    df = np.maximum(d, 1).astype(np.float64)
    large = max_exact + (np.log(df / max_exact) / math.log(MAX_DISTANCE / max_exact)
                         * (N_BUCKETS - max_exact)).astype(np.int32)
    large = np.minimum(large, N_BUCKETS - 1)
    return np.where(d < max_exact, d, large).astype(np.int32)


def _rms(x, g):
    return x * lax.rsqrt(jnp.mean(x * x, axis=-1, keepdims=True) + EPS) * g


def _dot(a, b):
    return jnp.dot(a, b, preferred_element_type=F32)


def _params(sem):
    return pltpu.CompilerParams(dimension_semantics=sem, vmem_limit_bytes=VMEM_LIMIT)


def _const_spec(shape):
    nd = len(shape)
    return pl.BlockSpec(shape, lambda *_: (0,) * nd)


def _inproj_kernel(x_ref, g_ref, wq_ref, wc_ref, wkv_ref, wg_ref, wx_ref,
                   q_ref, kcr_ref, vcr_ref, kv_ref, gt_ref, xgr_ref):
    h = _rms(x_ref[...], g_ref[...]).astype(BF16)
    q_ref[...] = (_dot(h, wq_ref[...]) * (HEAD_DIM ** -0.5 * LOG2E)).astype(BF16)
    c = _dot(h, wc_ref[...])
    kcr_ref[...] = c[:, :KV_WIDTH]
    vcr_ref[...] = c[:, KV_WIDTH:]
    kv_ref[...] = _dot(h, wkv_ref[...]).astype(BF16)
    gt_ref[...] = _dot(h, wg_ref[...])
    xgr_ref[...] = _dot(h, wx_ref[...])


def _inproj(x2, gain, wq, wc, wkv, wg, wx):
    n, d = x2.shape
    tm = ROW_TILE
    row = lambda w: pl.BlockSpec((tm, w), lambda i: (i, 0))
    return pl.pallas_call(
        _inproj_kernel,
        grid=(n // tm,),
        in_specs=[row(d), _const_spec(gain.shape), _const_spec(wq.shape), _const_spec(wc.shape),
                  _const_spec(wkv.shape), _const_spec(wg.shape), _const_spec(wx.shape)],
        out_specs=[row(NSA_WIDTH), row(KV_WIDTH), row(KV_WIDTH), row(4 * KV_WIDTH), row(128),
                   row(wx.shape[1])],
        out_shape=[jax.ShapeDtypeStruct((n, NSA_WIDTH), BF16),
                   jax.ShapeDtypeStruct((n, KV_WIDTH), F32),
                   jax.ShapeDtypeStruct((n, KV_WIDTH), F32),
                   jax.ShapeDtypeStruct((n, 4 * KV_WIDTH), BF16),
                   jax.ShapeDtypeStruct((n, 128), F32),
                   jax.ShapeDtypeStruct((n, wx.shape[1]), F32)],
        compiler_params=_params(("parallel",)),
        name="inproj",
    )(x2, gain, wq, wc, wkv, wg, wx)


def _compress_kernel(kr_ref, vr_ref, pek_ref, pev_ref, wk_ref, wv_ref, kc_ref, vc_ref):
    kc_ref[...] = _dot((kr_ref[...] + pek_ref[...]).astype(BF16), wk_ref[...]).astype(BF16)
    vc_ref[...] = _dot((vr_ref[...] + pev_ref[...]).astype(BF16), wv_ref[...]).astype(BF16)


def _compress(kr, vr, pek, pev, wk, wv):
    nblk, width = kr.shape
    tm = 128
    row = lambda w: pl.BlockSpec((tm, w), lambda i: (i, 0))
    return pl.pallas_call(
        _compress_kernel,
        grid=(nblk // tm,),
        in_specs=[row(width), row(width), _const_spec(pek.shape), _const_spec(pev.shape),
                  _const_spec(wk.shape), _const_spec(wv.shape)],
        out_specs=[row(KV_WIDTH), row(KV_WIDTH)],
        out_shape=[jax.ShapeDtypeStruct((nblk, KV_WIDTH), BF16)] * 2,
        compiler_params=_params(("parallel",)),
        name="compress",
    )(kr, vr, pek, pev, wk, wv)


def _bias_tables_kernel(rb_ref, bma_ref, bmb_ref, bm1_ref, bm2_ref,
                        ta_ref, tb_ref, d1_ref, d2_ref):
    last = N_BUCKETS - 1
    for hd in range(NSA_HEADS):
        g, hl = divmod(hd, NSA_HPG)
        cols = slice(hl * TQ, (hl + 1) * TQ)
        far = rb_ref[last, hd]
        for bm_ref, out_ref, masked in ((bma_ref, ta_ref, False), (bmb_ref, tb_ref, True),
                                        (bm1_ref, d1_ref, False), (bm2_ref, d2_ref, False)):
            bm = bm_ref[...]
            acc = jnp.zeros(bm.shape, F32)
            for k in range(N_BUCKETS - 1):
                acc = jnp.where(bm == k, (rb_ref[k, hd] - far) * LOG2E, acc)
            if masked:
                acc = jnp.where(bm < 0, MASKV, acc)
            out_ref[g, :, cols] = acc


def _bias_tables(rel_bias):
    ki = np.arange(CK)[:, None]
    qi = np.arange(TQ)[None, :]
    bma = _bucket_of_distance(qi - ki + CK)
    bmb = np.where(qi >= ki, _bucket_of_distance(qi - ki), -1).astype(np.int32)
    r = (np.arange(TQ) + 1) % CMP_BLOCK
    bm1 = np.broadcast_to(_bucket_of_distance(r)[None, :], (8, TQ)).astype(np.int32)
    bm2 = np.broadcast_to(_bucket_of_distance(r + CMP_BLOCK)[None, :], (8, TQ)).astype(np.int32)
    return pl.pallas_call(
        _bias_tables_kernel,
        in_specs=[pl.BlockSpec(memory_space=pltpu.SMEM)] + [pl.BlockSpec(memory_space=pltpu.VMEM)] * 4,
        out_specs=[pl.BlockSpec(memory_space=pltpu.VMEM)] * 4,
        out_shape=[jax.ShapeDtypeStruct((NSA_KV_HEADS, CK, ROWS), F32),
                   jax.ShapeDtypeStruct((NSA_KV_HEADS, CK, ROWS), F32),
                   jax.ShapeDtypeStruct((NSA_KV_HEADS, 8, ROWS), F32),
                   jax.ShapeDtypeStruct((NSA_KV_HEADS, 8, ROWS), F32)],
        name="bias_tables",
    )(rel_bias, jnp.asarray(bma), jnp.asarray(bmb), jnp.asarray(bm1), jnp.asarray(bm2))


def _attend_pass(chains, exact_items):
    items = [(ci, chunk) for k in range(max(len(c[0]) for c in chains))
             for ci, c in enumerate(chains) for chunk in c[0][k:k + 1]]

    def score(item):
        keys, q, bias, _ = item[1]
        s = _dot(keys, q)
        return s if bias is None else s + bias

    scores = [score(item) for item in items[:PIPE_DEPTH]]
    start = [(m_ref[...], acc_ref[...]) for _, m_ref, acc_ref in chains]
    state = list(start)
    tops = [None] * len(chains)
    seen = [0] * len(chains)
    for i, (ci, chunk) in enumerate(items):
        s = scores[i]
        m, acc = state[ci]
        top = jnp.max(s, axis=0, keepdims=True)
        if exact_items is None or seen[ci] < exact_items:
            m_new = jnp.maximum(m, top)
            p = jnp.exp2(s - m_new).astype(BF16)
            state[ci] = (m_new, jnp.exp2(m - m_new) * acc + _dot(chunk[3], p))
        else:
            tops[ci] = top if tops[ci] is None else jnp.maximum(tops[ci], top)
            state[ci] = (m, acc + _dot(chunk[3], jnp.exp2(s - m).astype(BF16)))
        seen[ci] += 1
        if i + PIPE_DEPTH < len(items):
            scores.append(score(items[i + PIPE_DEPTH]))
    rise = [top - m for top, (m, _) in zip(tops, state) if top is not None]
    moved = []
    for (m, acc), top in zip(state, tops):
        if top is not None:
            m_new = jnp.maximum(m, top)
            m, acc = m_new, jnp.exp2(m - m_new) * acc
        moved.append((m, acc))
    return moved, rise


def _commit(chains, state):
    for (_, m_ref, acc_ref), (m, acc) in zip(chains, state):
        m_ref[...] = m
        acc_ref[...] = acc


def _attend(chains):
    _commit(chains, _attend_pass(chains, exact_items=None)[0])


def _attend_from_reference(chains, exact_items=0):
    state, rise = _attend_pass(chains, exact_items)
    worst = rise[0]
    for r in rise[1:]:
        worst = jnp.maximum(worst, r)
    safe = jnp.max(worst) <= MAX_RISE

    @pl.when(safe)
    def _():
        _commit(chains, state)

    @pl.when(jnp.logical_not(safe))
    def _():
        _attend(chains)


def _attn_kernel(q_ref, gt_ref, kc_ref, vc_ref, ks_ref, vs_ref, kw_ref, vw_ref,
                 ta_ref, tb_ref, d1_ref, d2_ref, out_ref,
                 ka_ref, vst_ref, vwt_ref, vct_ref, qa_ref, oc_ref, ms_ref, accs_ref, mw_ref, accw_ref, outt_ref):
    qt = pl.program_id(1)
    n_blocks = kc_ref.shape[1]
    blocks_per_tile = TQ // CMP_BLOCK

    @pl.when(qt == 0)
    def _():
        ones_rows = jnp.where(lax.broadcasted_iota(jnp.int32, (VT_ROWS - HEAD_DIM, CK), 0) == 0,
                              1.0, 0.0).astype(BF16)

        def build(c, carry):
            rs = pl.ds(pl.multiple_of(c * CK, CK), CK)
            blk = lax.broadcasted_iota(jnp.int32, (CK, 128), 1)
            key_blk = c * (CK // CMP_BLOCK) + (lax.broadcasted_iota(jnp.int32, (CK, 128), 0) >> 6)
            ka_ref[rs, 0:128] = jnp.where(blk == key_blk, 1.0, 0.0).astype(BF16)
            ka_ref[rs, 128:256] = ks_ref[0, rs, :]
            vs_t = vs_ref[0, rs, :].astype(F32).T
            vw_t = vw_ref[0, rs, :].astype(F32).T
            for g in range(NSA_KV_HEADS):
                gs = slice(g * HEAD_DIM, (g + 1) * HEAD_DIM)
                vst_ref[c, g, 0:HEAD_DIM, :] = vs_t[gs].astype(BF16)
                vst_ref[c, g, HEAD_DIM:, :] = ones_rows
                vwt_ref[c, g, 0:HEAD_DIM, :] = vw_t[gs].astype(BF16)
                vwt_ref[c, g, HEAD_DIM:, :] = ones_rows
            return carry
        lax.fori_loop(0, ks_ref.shape[1] // CK, build, 0)
        vct_ref[...] = vc_ref[0].astype(F32).T.astype(BF16)

    q_t = q_ref[0].astype(F32).T
    gates_t = jax.nn.sigmoid(gt_ref[0]).T
    kc = kc_ref[0]

    q_loc = lax.broadcasted_iota(jnp.int32, (1, ROWS), 1) & (TQ - 1)
    n_iota = lax.broadcasted_iota(jnp.int32, (n_blocks, ROWS), 0)
    n1 = qt * blocks_per_tile + ((q_loc + 1) >> 6) - 1
    j_s = lax.broadcasted_iota(jnp.int32, (n_blocks, TQ), 0)
    j_f = j_s.astype(F32)
    cur = qt * blocks_per_tile + (lax.broadcasted_iota(jnp.int32, (1, TQ), 1) >> 6)
    zeros_half = jnp.zeros((HEAD_DIM, ROWS), BF16)

    def chunk_rows(c):
        return pl.ds(pl.multiple_of(c * CK, CK), CK)

    for g in range(NSA_KV_HEADS):
        gs = slice(g * HEAD_DIM, (g + 1) * HEAD_DIM)
        q_g = jnp.concatenate(
            [q_t[(g * NSA_HPG + hl) * HEAD_DIM:(g * NSA_HPG + hl + 1) * HEAD_DIM, :]
             for hl in range(NSA_HPG)], axis=1).astype(BF16)
        q_both = jnp.concatenate([q_g, zeros_half] if g == 0 else [zeros_half, q_g], axis=0)

        s_c = _dot(kc, q_both)
        bias_c = (jnp.where(n_iota == n1, d1_ref[g, 0:1, :], 0.0)
                  + jnp.where(n_iota == n1 - 1, d2_ref[g, 0:1, :], 0.0))
        valid_c = n_iota <= n1
        l_c = jnp.where(valid_c, s_c + bias_c, NEG)
        m_c = jnp.max(l_c, axis=0, keepdims=True)
        e_c = jnp.where(valid_c, jnp.exp2(l_c - m_c), 0.0)
        p_c = e_c / jnp.maximum(jnp.sum(e_c, axis=0, keepdims=True), 1e-30)
        oc_ref[g] = _dot(vct_ref[gs, :], p_c.astype(BF16))

        imp = p_c[:, 0:TQ]
        for hl in range(1, NSA_HPG):
            imp = imp + p_c[:, hl * TQ:(hl + 1) * TQ]
        forced = (j_s == 0) | (j_s == cur) | (j_s == cur - 1)
        candidates = jnp.where(forced, NEG, jnp.where(j_s <= cur, imp, NEG))
        work = candidates
        for _ in range(SEL_TOPN - 3):
            mx = jnp.max(work, axis=0, keepdims=True)
            first = jnp.min(jnp.where(work == mx, j_f, float(n_blocks)), axis=0, keepdims=True)
            work = jnp.where(j_f == first, -jnp.inf, work)
        picked_mask = jnp.where(candidates > NEG / 2, jnp.where(work == -jnp.inf, 0.0, MASKV), MASKV)
        sel_mask = jnp.where(forced, 0.0, picked_mask).astype(BF16)
        qa_ref[g, 0:128, :] = jnp.concatenate([sel_mask] * NSA_HPG, axis=1)
        qa_ref[g, 128:256, :] = q_both

        for m_ref, acc_ref in ((ms_ref, accs_ref), (mw_ref, accw_ref)):
            m_ref[g] = jnp.full((1, ROWS), M_INIT, F32)
            acc_ref[g] = jnp.zeros((VT_ROWS, ROWS), F32)

    groups = range(NSA_KV_HEADS)

    def sel_chains(cs, biases=None):
        keys = [ka_ref[chunk_rows(c), :] for c in cs]
        biases = biases or [None] * len(cs)
        return [([(k, qa_ref[g], bias if bias is None else bias[g], vst_ref[c, g])
                  for k, c, bias in zip(keys, cs, biases)], ms_ref.at[g], accs_ref.at[g]) for g in groups]

    def win_chains(cs, biases):
        keys = [kw_ref[0, chunk_rows(c), :] for c in cs]
        return [([(k, qa_ref[g, 128:256, :], bias[g], vwt_ref[c, g])
                  for k, c, bias in zip(keys, cs, biases)], mw_ref.at[g], accw_ref.at[g]) for g in groups]

    @pl.when(qt >= 2)
    def _():
        w2_mask = jnp.where(q_loc < lax.broadcasted_iota(jnp.int32, (CK, ROWS), 0), 0.0, MASKV)
        ta, tb = [ta_ref[g] for g in groups], [tb_ref[g] for g in groups]
        _attend_from_reference(sel_chains([qt, qt - 1], [tb, ta])
                               + win_chains([qt, qt - 1, qt - 2], [tb, ta, [w2_mask] * NSA_KV_HEADS]), exact_items=1)

    @pl.when(qt == 1)
    def _():
        ta, tb = [ta_ref[g] for g in groups], [tb_ref[g] for g in groups]
        _attend_from_reference(sel_chains([1, 0], [tb, ta]) + win_chains([1, 0], [tb, ta]), exact_items=1)

    @pl.when(qt == 0)
    def _():
        tb = [tb_ref[g] for g in groups]
        _attend(sel_chains([0], [tb]) + win_chains([0], [tb]))

    n_far = jnp.maximum(qt - 1, 0)

    def far_block(i, carry):
        _attend_from_reference(sel_chains([FAR_BLOCK * i + k for k in range(FAR_BLOCK)]))
        return carry
    lax.fori_loop(0, n_far // FAR_BLOCK, far_block, 0)
    done = (n_far // FAR_BLOCK) * FAR_BLOCK
    size = FAR_BLOCK // 2
    while size:
        @pl.when((n_far & size) != 0)
        def _(done=done, size=size):
            _attend_from_reference(sel_chains([done + k for k in range(size)]))
        done = done + (n_far & size)
        size //= 2

    for g in range(NSA_KV_HEADS):
        o_s = accs_ref[g, 0:HEAD_DIM, :] / accs_ref[g, HEAD_DIM:HEAD_DIM + 1, :]
        o_w = accw_ref[g, 0:HEAD_DIM, :] / accw_ref[g, HEAD_DIM:HEAD_DIM + 1, :]

        def gate(branch):
            return jnp.concatenate(
                [gates_t[(g * NSA_HPG + hl) * 3 + branch:(g * NSA_HPG + hl) * 3 + branch + 1, :]
                 for hl in range(NSA_HPG)], axis=1)
        mixed = gate(0) * oc_ref[g] + gate(1) * o_s + gate(2) * o_w
        for hl in range(NSA_HPG):
            hd = g * NSA_HPG + hl
            outt_ref[hd * HEAD_DIM:(hd + 1) * HEAD_DIM, :] = mixed[:, hl * TQ:(hl + 1) * TQ]

    out_ref[0] = outt_ref[...].T


def _attention(q, gates, kc, vc, kv, tabs):
    b, s, _ = q.shape
    n_blocks = s // CMP_BLOCK
    nch = s // CK
    ta, tb, d1, d2 = tabs
    tile = lambda w: pl.BlockSpec((1, TQ, w), lambda bi, qi: (bi, qi, 0))
    per_b = lambda r, w, col: pl.BlockSpec((1, r, w), lambda bi, qi: (bi, 0, col), pipeline_mode=pl.Buffered(1))
    return pl.pallas_call(
        _attn_kernel,
        grid=(b, s // TQ),
        in_specs=[tile(NSA_WIDTH), tile(128),
                  per_b(n_blocks, KV_WIDTH, 0), per_b(n_blocks, KV_WIDTH, 0),
                  per_b(s, KV_WIDTH, 0), per_b(s, KV_WIDTH, 1), per_b(s, KV_WIDTH, 2), per_b(s, KV_WIDTH, 3),
                  _const_spec(ta.shape), _const_spec(tb.shape), _const_spec(d1.shape), _const_spec(d2.shape)],
        out_specs=tile(NSA_WIDTH),
        out_shape=jax.ShapeDtypeStruct((b, s, NSA_WIDTH), F32),
        scratch_shapes=[pltpu.VMEM((s, CK), BF16),
                        pltpu.VMEM((nch, NSA_KV_HEADS, VT_ROWS, CK), BF16),
                        pltpu.VMEM((nch, NSA_KV_HEADS, VT_ROWS, CK), BF16),
                        pltpu.VMEM((KV_WIDTH, n_blocks), BF16),
                        pltpu.VMEM((NSA_KV_HEADS, CK, ROWS), BF16),
                        pltpu.VMEM((NSA_KV_HEADS, HEAD_DIM, ROWS), F32),
                        pltpu.VMEM((NSA_KV_HEADS, 1, ROWS), F32), pltpu.VMEM((NSA_KV_HEADS, VT_ROWS, ROWS), F32),
                        pltpu.VMEM((NSA_KV_HEADS, 1, ROWS), F32), pltpu.VMEM((NSA_KV_HEADS, VT_ROWS, ROWS), F32),
                        pltpu.VMEM((NSA_WIDTH, TQ), F32)],
        compiler_params=_params(("parallel", "arbitrary")),
        name="nsa_attention",
    )(q, gates, kc, vc, kv, kv, kv, kv, ta, tb, d1, d2)


def _lru_kernel(xgr_ref, cw_ref, cb_ref, wa_ref, ba_ref, wx_ref, bx_ref, lam_ref,
                out_ref, xbuf_ref, h_ref):
    t = pl.program_id(1)
    ts = LRU_TILE
    w = out_ref.shape[2]

    @pl.when(t == 0)
    def _():
        xbuf_ref[0:8, :] = jnp.zeros((8, w), F32)
        h_ref[...] = jnp.zeros_like(h_ref)

    xg = xgr_ref[0, :, :w]
    xbuf_ref[8:, :] = xgr_ref[0, :, w:]
    xc = cb_ref[...] + xbuf_ref[8:, :] * cw_ref[CONV_WIDTH - 1:CONV_WIDTH, :]
    for k in range(1, CONV_WIDTH):
        xc = xc + xbuf_ref[8 - k:8 - k + ts, :] * cw_ref[CONV_WIDTH - 1 - k:CONV_WIDTH - k, :]
    xbuf_ref[0:8, :] = xbuf_ref[ts:ts + 8, :]

    xcb = xc.astype(BF16)
    r = jax.nn.sigmoid(_dot(xcb, wa_ref[...]) + ba_ref[...])
    i = jax.nn.sigmoid(_dot(xcb, wx_ref[...]) + bx_ref[...])
    nl = -lam_ref[...]
    softplus = jnp.maximum(nl, 0.0) + jnp.log(1.0 + jnp.exp(-jnp.abs(nl)))
    log_a = (-LRU_C * softplus) * r
    a = jnp.exp(log_a)
    bv = jnp.sqrt(1.0 - a * a) * (i * xc)

    row8 = lax.broadcasted_iota(jnp.int32, (ts, w), 0) & 7
    sh = 1
    while sh < 8:
        a_prev = jnp.where(row8 >= sh, pltpu.roll(a, sh, 0), 1.0)
        b_prev = jnp.where(row8 >= sh, pltpu.roll(bv, sh, 0), 0.0)
        bv = a * b_prev + bv
        a = a * a_prev
        sh *= 2
    y = jax.nn.gelu(xg)
    hc = h_ref[0:1, :]
    for grp in range(ts // 8):
        rs = slice(8 * grp, 8 * grp + 8)
        hg = bv[rs] + a[rs] * hc
        out_ref[0, rs, :] = hg * y[rs]
        hc = hg[7:8]
    h_ref[0:1, :] = hc


def _lru(xgr, cw, cb, wa, ba, wx, bx, lam):
    b, s, w2 = xgr.shape
    w = w2 // 2
    ts = LRU_TILE
    return pl.pallas_call(
        _lru_kernel,
        grid=(b, s // ts),
        in_specs=[pl.BlockSpec((1, ts, w2), lambda bi, ti: (bi, ti, 0)),
                  _const_spec(cw.shape), _const_spec(cb.shape), _const_spec(wa.shape), _const_spec(ba.shape),
                  _const_spec(wx.shape), _const_spec(bx.shape), _const_spec(lam.shape)],
        out_specs=pl.BlockSpec((1, ts, w), lambda bi, ti: (bi, ti, 0)),
        out_shape=jax.ShapeDtypeStruct((b, s, w), F32),
        scratch_shapes=[pltpu.VMEM((ts + 8, w), F32), pltpu.VMEM((8, w), F32)],
        compiler_params=_params(("parallel", "arbitrary")),
        name="rg_lru",
    )(xgr, cw, cb, wa, ba, wx, bx, lam)


def _post_kernel(x_ref, att_ref, lru_ref, kt_ref, v_ref,
                 ga_ref, gl_ref, wa_ref, wl_ref, gmix_ref,
                 gxpre_ref, wq_ref, wo_ref, gxpost_ref,
                 gmpre_ref, w1_ref, w2_ref, gmpost_ref, out_ref):
    tm, d = x_ref.shape[1], x_ref.shape[2]
    parts = [slice(k * tm // ROW_SPLIT, (k + 1) * tm // ROW_SPLIT) for k in range(ROW_SPLIT)]
    xs = [x_ref[0, r, :] for r in parts]
    a = [_rms(att_ref[0, r, :], ga_ref[...]).astype(BF16) for r in parts]
    l = [_rms(lru_ref[0, r, :], gl_ref[...]).astype(BF16) for r in parts]
    mixed = [_dot(ak, wa_ref[...]) + _dot(lk, wl_ref[...]) for ak, lk in zip(a, l)]
    xs = [x + _rms(mk, gmix_ref[...]) for x, mk in zip(xs, mixed)]

    dh = d // X_HEADS
    hq = [_rms(x, gxpre_ref[...]).astype(BF16) for x in xs]
    cq = [(_dot(h, wq_ref[...]) * (dh ** -0.5)).astype(BF16) for h in hq]
    outs = [[] for _ in parts]
    for hh in range(X_HEADS):
        hs = slice(hh * dh, (hh + 1) * dh)
        scores = [_dot(c[:, hs], kt_ref[0, hs, :]) for c in cq]
        for out, s in zip(outs, scores):
            e = jnp.exp(s - jnp.max(s, axis=1, keepdims=True))
            p = e / jnp.sum(e, axis=1, keepdims=True)
            out.append(_dot(p.astype(BF16), v_ref[0, :, hs]).astype(BF16))
    co = [_dot(jnp.concatenate(out, axis=1), wo_ref[...]) for out in outs]
    xs = [x + _rms(c, gxpost_ref[...]) for x, c in zip(xs, co)]

    hm = [_rms(x, gmpre_ref[...]).astype(BF16) for x in xs]
    accs = [jnp.zeros(x.shape, F32) for x in xs]
    for c in range(w1_ref.shape[1] // FF_CHUNK):
        cs = slice(c * FF_CHUNK, (c + 1) * FF_CHUNK)
        us = [jnp.maximum(_dot(h, w1_ref[:, cs]), 0.0) for h in hm]
        accs = [acc + _dot((u * u).astype(BF16), w2_ref[cs, :]) for acc, u in zip(accs, us)]
    for r, x, acc in zip(parts, xs, accs):
        out_ref[0, r, :] = x + _rms(acc, gmpost_ref[...])


def _post(x, att, lru, kt, v, gains_and_weights):
    b, s, d = x.shape
    m = v.shape[1]
    tm = ROW_TILE
    row = lambda w: pl.BlockSpec((1, tm, w), lambda bi, i: (bi, i, 0))
    resident = lambda a: pl.BlockSpec(a.shape, lambda *_: (0,) * a.ndim, pipeline_mode=pl.Buffered(1))
    return pl.pallas_call(
        _post_kernel,
        grid=(b, s // tm),
        in_specs=[row(d), row(att.shape[2]), row(lru.shape[2]),
                  pl.BlockSpec((1, d, m), lambda bi, i: (bi, 0, 0)),
                  pl.BlockSpec((1, m, d), lambda bi, i: (bi, 0, 0))]
                 + [resident(a) for a in gains_and_weights],
        out_specs=row(d),
        out_shape=jax.ShapeDtypeStruct((b, s, d), F32),
        compiler_params=_params(("parallel", "parallel")),
        name="post_mixer",
    )(x, att, lru, kt, v, *gains_and_weights)


def _memkv_kernel(mem_ref, g_ref, wk_ref, wv_ref, kt_ref, v_ref):
    mn = _rms(mem_ref[0], g_ref[...]).astype(BF16)
    kt_ref[0] = _dot(mn, wk_ref[...]).T.astype(BF16)
    v_ref[0] = _dot(mn, wv_ref[...]).astype(BF16)


def _memkv(mem, g, wk, wv):
    b, m, d = mem.shape
    return pl.pallas_call(
        _memkv_kernel,
        grid=(b,),
        in_specs=[pl.BlockSpec((1, m, d), lambda i: (i, 0, 0)), _const_spec(g.shape),
                  _const_spec(wk.shape), _const_spec(wv.shape)],
        out_specs=[pl.BlockSpec((1, d, m), lambda i: (i, 0, 0)), pl.BlockSpec((1, m, d), lambda i: (i, 0, 0))],
        out_shape=[jax.ShapeDtypeStruct((b, d, m), BF16), jax.ShapeDtypeStruct((b, m, d), BF16)],
        compiler_params=_params(("parallel",)),
        name="mem_kv",
    )(mem, g, wk, wv)


def _block_structured(w):
    eye = jnp.eye(NSA_KV_HEADS, dtype=w.dtype)
    big = w[:, None, :, None, :] * eye[None, :, None, :, None]
    return big.reshape(CMP_BLOCK * KV_WIDTH, KV_WIDTH).astype(BF16)


def _block_diag(w):
    n, d, e = w.shape
    eye = jnp.eye(n, dtype=w.dtype)
    return (w[:, :, None, :] * eye[:, None, :, None]).reshape(n * d, n * e).astype(BF16)


def kernel(x, mem, rel_bias, ln_mix_pre, ln_mix_post, w_in, cmp_pe_k, cmp_pe_v, cmp_w_k, cmp_w_v,
           conv_w, conv_b, lru_wa, lru_ba, lru_wx, lru_bx, lru_lambda, gn_attn, gn_lru, w_out,
           ln_x_pre, ln_x_post, ln_mem, xq, xkv, xo, ln_mlp_pre, ln_mlp_post, mlp_w1, mlp_w2):
    b, s, d = x.shape
    depth = w_in.shape[0]
    n = b * s
    lru_w = conv_w.shape[2]
    assert s % TQ == 0 and s % LRU_TILE == 0 and n % ROW_TILE == 0
    assert s // CMP_BLOCK == 128, "selection mask layout assumes 128 compression blocks"
    row1 = lambda v: v.reshape(1, -1)

    tabs = _bias_tables(rel_bias)
    x2 = x.reshape(n, d)
    o_q = NSA_WIDTH
    o_g = o_q + 6 * KV_WIDTH
    o_x = o_g + 3 * NSA_HEADS
    for l in range(depth):
        wl = w_in[l]
        wq = wl[:, :o_q].astype(BF16)
        wc = wl[:, o_q:o_q + 2 * KV_WIDTH].astype(BF16)
        wkv = wl[:, o_q + 2 * KV_WIDTH:o_g].astype(BF16)
        wg = jnp.pad(wl[:, o_g:o_x], ((0, 0), (0, 128 - 3 * NSA_HEADS))).astype(BF16)
        wx = wl[:, o_x:].astype(BF16)
        q, kcr, vcr, kv, gt, xgr = _inproj(x2, row1(ln_mix_pre[l]), wq, wc, wkv, wg, wx)

        nblk = n // CMP_BLOCK
        pek = jnp.tile(cmp_pe_k[l], (1, NSA_KV_HEADS)).reshape(1, -1)
        pev = jnp.tile(cmp_pe_v[l], (1, NSA_KV_HEADS)).reshape(1, -1)
        kc, vc = _compress(kcr.reshape(nblk, CMP_BLOCK * KV_WIDTH), vcr.reshape(nblk, CMP_BLOCK * KV_WIDTH),
                           pek, pev, _block_structured(cmp_w_k[l]), _block_structured(cmp_w_v[l]))

        att = _attention(q.reshape(b, s, NSA_WIDTH), gt.reshape(b, s, 128),
                         kc.reshape(b, s // CMP_BLOCK, KV_WIDTH), vc.reshape(b, s // CMP_BLOCK, KV_WIDTH),
                         kv.reshape(b, s, 4 * KV_WIDTH), tabs)
        lru = _lru(xgr.reshape(b, s, 2 * lru_w), conv_w[l], row1(conv_b[l]),
                   _block_diag(lru_wa[l]), row1(lru_ba[l]), _block_diag(lru_wx[l]), row1(lru_bx[l]),
                   row1(lru_lambda[l]))

        wo = w_out[l].astype(BF16)
        wkv_x = xkv[l].astype(BF16)
        kt, v = _memkv(mem, row1(ln_mem[l]), wkv_x[:, :d], wkv_x[:, d:])
        x2 = _post(x2.reshape(b, s, d), att, lru, kt, v,
                   (row1(gn_attn[l]), row1(gn_lru[l]), wo[:NSA_WIDTH], wo[NSA_WIDTH:], row1(ln_mix_post[l]),
                    row1(ln_x_pre[l]), xq[l].astype(BF16), xo[l].astype(BF16), row1(ln_x_post[l]),
                    row1(ln_mlp_pre[l]), mlp_w1[l].astype(BF16), mlp_w2[l].astype(BF16),
                    row1(ln_mlp_post[l]))).reshape(n, d)
    return x2.reshape(b, s, d)
```

```python
import functools
import math

import numpy as np
import jax
import jax.numpy as jnp
from jax import lax
from jax.experimental import pallas as pl
from jax.experimental.pallas import tpu as pltpu

HEAD_DIM = 64
NSA_HEADS = 8
NSA_KV_HEADS = 2
NSA_HPG = NSA_HEADS // NSA_KV_HEADS
NSA_WIDTH = NSA_HEADS * HEAD_DIM
KV_WIDTH = NSA_KV_HEADS * HEAD_DIM
CMP_BLOCK = 64
SEL_TOPN = 16
WINDOW = 512
FORCE_SCORE = 1e4
LRU_BLOCKS = 8
CONV_WIDTH = 4
LRU_C = 8.0
X_HEADS = 4
N_BUCKETS = 32
MAX_DISTANCE = 128
EPS = 1e-6
NEG = -1e30

TQ = 256
CK = 256
ROWS = NSA_HPG * TQ
VT_ROWS = HEAD_DIM + 16
LOG2E = 1.4426950408889634
FAR_BLOCK = 8
PIPE_DEPTH = 4
MASKV = -(2.0 ** 100)
MAX_RISE = 64.0
ROW_TILE = 512
FF_CHUNK = 1024
ROW_SPLIT = 2
LRU_TILE = 512
VMEM_LIMIT = 56 * 1024 * 1024

F32 = jnp.float32
BF16 = jnp.bfloat16


def _bucket_of_distance(d):
    max_exact = N_BUCKETS // 2
    d = np.maximum(d, 0)
    df = np.maximum(d, 1).astype(np.float64)
    large = max_exact + (np.log(df / max_exact) / math.log(MAX_DISTANCE / max_exact)
                         * (N_BUCKETS - max_exact)).astype(np.int32)
    large = np.minimum(large, N_BUCKETS - 1)
    return np.where(d < max_exact, d, large).astype(np.int32)


def _rms(x, g):
    return x * lax.rsqrt(jnp.mean(x * x, axis=-1, keepdims=True) + EPS) * g


def _dot(a, b):
    return jnp.dot(a, b, preferred_element_type=F32)


def _params(sem):
    return pltpu.CompilerParams(dimension_semantics=sem, vmem_limit_bytes=VMEM_LIMIT)


def _const_spec(shape):
    nd = len(shape)
    return pl.BlockSpec(shape, lambda *_: (0,) * nd)


def _inproj_kernel(x_ref, g_ref, wq_ref, wc_ref, wkv_ref, wg_ref, wx_ref,
                   q_ref, kcr_ref, vcr_ref, kv_ref, gt_ref, xgr_ref):
    h = _rms(x_ref[...], g_ref[...]).astype(BF16)
    q_ref[...] = (_dot(h, wq_ref[...]) * (HEAD_DIM ** -0.5 * LOG2E)).astype(BF16)
    c = _dot(h, wc_ref[...])
    kcr_ref[...] = c[:, :KV_WIDTH]
    vcr_ref[...] = c[:, KV_WIDTH:]
    kv_ref[...] = _dot(h, wkv_ref[...]).astype(BF16)
    gt_ref[...] = _dot(h, wg_ref[...])
    xgr_ref[...] = _dot(h, wx_ref[...])


def _inproj(x2, gain, wq, wc, wkv, wg, wx):
    n, d = x2.shape
    tm = ROW_TILE
    row = lambda w: pl.BlockSpec((tm, w), lambda i: (i, 0))
    return pl.pallas_call(
        _inproj_kernel,
        grid=(n // tm,),
        in_specs=[row(d), _const_spec(gain.shape), _const_spec(wq.shape), _const_spec(wc.shape),
                  _const_spec(wkv.shape), _const_spec(wg.shape), _const_spec(wx.shape)],
        out_specs=[row(NSA_WIDTH), row(KV_WIDTH), row(KV_WIDTH), row(4 * KV_WIDTH), row(128),
                   row(wx.shape[1])],
        out_shape=[jax.ShapeDtypeStruct((n, NSA_WIDTH), BF16),
                   jax.ShapeDtypeStruct((n, KV_WIDTH), F32),
                   jax.ShapeDtypeStruct((n, KV_WIDTH), F32),
                   jax.ShapeDtypeStruct((n, 4 * KV_WIDTH), BF16),
                   jax.ShapeDtypeStruct((n, 128), F32),
                   jax.ShapeDtypeStruct((n, wx.shape[1]), F32)],
        compiler_params=_params(("parallel",)),
        name="inproj",
    )(x2, gain, wq, wc, wkv, wg, wx)


def _compress_kernel(kr_ref, vr_ref, pek_ref, pev_ref, wk_ref, wv_ref, kc_ref, vc_ref):
    kc_ref[...] = _dot((kr_ref[...] + pek_ref[...]).astype(BF16), wk_ref[...]).astype(BF16)
    vc_ref[...] = _dot((vr_ref[...] + pev_ref[...]).astype(BF16), wv_ref[...]).astype(BF16)


def _compress(kr, vr, pek, pev, wk, wv):
    nblk, width = kr.shape
    tm = 128
    row = lambda w: pl.BlockSpec((tm, w), lambda i: (i, 0))
    return pl.pallas_call(
        _compress_kernel,
        grid=(nblk // tm,),
        in_specs=[row(width), row(width), _const_spec(pek.shape), _const_spec(pev.shape),
                  _const_spec(wk.shape), _const_spec(wv.shape)],
        out_specs=[row(KV_WIDTH), row(KV_WIDTH)],
        out_shape=[jax.ShapeDtypeStruct((nblk, KV_WIDTH), BF16)] * 2,
        compiler_params=_params(("parallel",)),
        name="compress",
    )(kr, vr, pek, pev, wk, wv)


def _bias_tables_kernel(rb_ref, bma_ref, bmb_ref, bm1_ref, bm2_ref,
                        ta_ref, tb_ref, d1_ref, d2_ref):
    last = N_BUCKETS - 1
    for hd in range(NSA_HEADS):
        g, hl = divmod(hd, NSA_HPG)
        cols = slice(hl * TQ, (hl + 1) * TQ)
        far = rb_ref[last, hd]
        for bm_ref, out_ref, masked in ((bma_ref, ta_ref, False), (bmb_ref, tb_ref, True),
                                        (bm1_ref, d1_ref, False), (bm2_ref, d2_ref, False)):
            bm = bm_ref[...]
            acc = jnp.zeros(bm.shape, F32)
            for k in range(N_BUCKETS - 1):
                acc = jnp.where(bm == k, (rb_ref[k, hd] - far) * LOG2E, acc)
            if masked:
                acc = jnp.where(bm < 0, MASKV, acc)
            out_ref[g, :, cols] = acc


def _bias_tables(rel_bias):
    ki = np.arange(CK)[:, None]
    qi = np.arange(TQ)[None, :]
    bma = _bucket_of_distance(qi - ki + CK)
    bmb = np.where(qi >= ki, _bucket_of_distance(qi - ki), -1).astype(np.int32)
    r = (np.arange(TQ) + 1) % CMP_BLOCK
    bm1 = np.broadcast_to(_bucket_of_distance(r)[None, :], (8, TQ)).astype(np.int32)
    bm2 = np.broadcast_to(_bucket_of_distance(r + CMP_BLOCK)[None, :], (8, TQ)).astype(np.int32)
    return pl.pallas_call(
        _bias_tables_kernel,
        in_specs=[pl.BlockSpec(memory_space=pltpu.SMEM)] + [pl.BlockSpec(memory_space=pltpu.VMEM)] * 4,
        out_specs=[pl.BlockSpec(memory_space=pltpu.VMEM)] * 4,
        out_shape=[jax.ShapeDtypeStruct((NSA_KV_HEADS, CK, ROWS), F32),
                   jax.ShapeDtypeStruct((NSA_KV_HEADS, CK, ROWS), F32),
                   jax.ShapeDtypeStruct((NSA_KV_HEADS, 8, ROWS), F32),
                   jax.ShapeDtypeStruct((NSA_KV_HEADS, 8, ROWS), F32)],
        name="bias_tables",
    )(rel_bias, jnp.asarray(bma), jnp.asarray(bmb), jnp.asarray(bm1), jnp.asarray(bm2))


def _attend_pass(chains, exact_items):
    items = [(ci, chunk) for k in range(max(len(c[0]) for c in chains))
             for ci, c in enumerate(chains) for chunk in c[0][k:k + 1]]

    def score(item):
        keys, q, bias, _ = item[1]
        s = _dot(keys, q)
        return s if bias is None else s + bias

    scores = [score(item) for item in items[:PIPE_DEPTH]]
    start = [(m_ref[...], acc_ref[...]) for _, m_ref, acc_ref in chains]
    state = list(start)
    tops = [None] * len(chains)
    seen = [0] * len(chains)
    for i, (ci, chunk) in enumerate(items):
        s = scores[i]
        m, acc = state[ci]
        top = jnp.max(s, axis=0, keepdims=True)
        if exact_items is None or seen[ci] < exact_items:
            m_new = jnp.maximum(m, top)
            p = jnp.exp2(s - m_new).astype(BF16)
            state[ci] = (m_new, jnp.exp2(m - m_new) * acc + _dot(chunk[3], p))
        else:
            tops[ci] = top if tops[ci] is None else jnp.maximum(tops[ci], top)
            state[ci] = (m, acc + _dot(chunk[3], jnp.exp2(s - m).astype(BF16)))
        seen[ci] += 1
        if i + PIPE_DEPTH < len(items):
            scores.append(score(items[i + PIPE_DEPTH]))
    rise = [top - m for top, (m, _) in zip(tops, state) if top is not None]
    moved = []
    for (m, acc), top in zip(state, tops):
        if top is not None:
            m_new = jnp.maximum(m, top)
            m, acc = m_new, jnp.exp2(m - m_new) * acc
        moved.append((m, acc))
    return moved, rise


def _commit(chains, state):
    for (_, m_ref, acc_ref), (m, acc) in zip(chains, state):
        m_ref[...] = m
        acc_ref[...] = acc


def _attend(chains):
    _commit(chains, _attend_pass(chains, None)[0])


def _attend_from_reference(chains, exact_items=0):
    state, rise = _attend_pass(chains, exact_items)
    worst = rise[0]
    for r in rise[1:]:
        worst = jnp.maximum(worst, r)
    safe = jnp.max(worst) <= MAX_RISE

    @pl.when(safe)
    def _():
        _commit(chains, state)

    @pl.when(jnp.logical_not(safe))
    def _():
        _attend(chains)


def _attn_kernel(q_ref, gt_ref, kc_ref, vc_ref, ks_ref, vs_ref, kw_ref, vw_ref,
                 ta_ref, tb_ref, d1_ref, d2_ref, out_ref,
                 ka_ref, vst_ref, vwt_ref, vct_ref, qa_ref, oc_ref, ms_ref, accs_ref, mw_ref, accw_ref, outt_ref):
    qt = pl.program_id(1)
    n_blocks = kc_ref.shape[1]
    blocks_per_tile = TQ // CMP_BLOCK

    @pl.when(qt == 0)
    def _():
        ones_rows = jnp.where(lax.broadcasted_iota(jnp.int32, (VT_ROWS - HEAD_DIM, CK), 0) == 0,
                              1.0, 0.0).astype(BF16)

        def build(c, carry):
            rs = pl.ds(pl.multiple_of(c * CK, CK), CK)
            blk = lax.broadcasted_iota(jnp.int32, (CK, 128), 1)
            key_blk = c * (CK // CMP_BLOCK) + (lax.broadcasted_iota(jnp.int32, (CK, 128), 0) >> 6)
            ka_ref[rs, 0:128] = jnp.where(blk == key_blk, 1.0, 0.0).astype(BF16)
            ka_ref[rs, 128:256] = ks_ref[0, rs, :]
            vs_t = vs_ref[0, rs, :].astype(F32).T
            vw_t = vw_ref[0, rs, :].astype(F32).T
            for g in range(NSA_KV_HEADS):
                gs = slice(g * HEAD_DIM, (g + 1) * HEAD_DIM)
                vst_ref[c, g, 0:HEAD_DIM, :] = vs_t[gs].astype(BF16)
                vst_ref[c, g, HEAD_DIM:, :] = ones_rows
                vwt_ref[c, g, 0:HEAD_DIM, :] = vw_t[gs].astype(BF16)
                vwt_ref[c, g, HEAD_DIM:, :] = ones_rows
            return carry
        lax.fori_loop(0, ks_ref.shape[1] // CK, build, 0)
        vct_ref[...] = vc_ref[0].astype(F32).T.astype(BF16)

    q_t = q_ref[0].astype(F32).T
    gates_t = jax.nn.sigmoid(gt_ref[0]).T
    kc = kc_ref[0]
    own_rows = pl.ds(pl.multiple_of(qt * CK, CK), CK)
    own_ks_t = ka_ref[own_rows, 128:256].astype(F32).T
    own_kw_t = kw_ref[0, own_rows, :].astype(F32).T

    q_loc = lax.broadcasted_iota(jnp.int32, (1, ROWS), 1) & (TQ - 1)
    n_iota = lax.broadcasted_iota(jnp.int32, (n_blocks, ROWS), 0)
    n1 = qt * blocks_per_tile + ((q_loc + 1) >> 6) - 1
    j_s = lax.broadcasted_iota(jnp.int32, (n_blocks, TQ), 0)
    j_f = j_s.astype(F32)
    cur = qt * blocks_per_tile + (lax.broadcasted_iota(jnp.int32, (1, TQ), 1) >> 6)
    zeros_half = jnp.zeros((HEAD_DIM, ROWS), BF16)

    def chunk_rows(c):
        return pl.ds(pl.multiple_of(c * CK, CK), CK)

    for g in range(NSA_KV_HEADS):
        gs = slice(g * HEAD_DIM, (g + 1) * HEAD_DIM)
        q_g = jnp.concatenate(
            [q_t[(g * NSA_HPG + hl) * HEAD_DIM:(g * NSA_HPG + hl + 1) * HEAD_DIM, :]
             for hl in range(NSA_HPG)], axis=1).astype(BF16)
        q_both = jnp.concatenate([q_g, zeros_half] if g == 0 else [zeros_half, q_g], axis=0)

        s_c = _dot(kc, q_both)
        bias_c = (jnp.where(n_iota == n1, d1_ref[g, 0:1, :], 0.0)
                  + jnp.where(n_iota == n1 - 1, d2_ref[g, 0:1, :], 0.0))
        valid_c = n_iota <= n1
        l_c = jnp.where(valid_c, s_c + bias_c, NEG)
        m_c = jnp.max(l_c, axis=0, keepdims=True)
        e_c = jnp.where(valid_c, jnp.exp2(l_c - m_c), 0.0)
        p_c = e_c / jnp.maximum(jnp.sum(e_c, axis=0, keepdims=True), 1e-30)
        oc_ref[g] = _dot(vct_ref[gs, :], p_c.astype(BF16))

        imp = p_c[:, 0:TQ]
        for hl in range(1, NSA_HPG):
            imp = imp + p_c[:, hl * TQ:(hl + 1) * TQ]
        forced = (j_s == 0) | (j_s == cur) | (j_s == cur - 1)
        candidates = jnp.where(forced, NEG, jnp.where(j_s <= cur, imp, NEG))
        work = candidates
        for _ in range(SEL_TOPN - 3):
            mx = jnp.max(work, axis=0, keepdims=True)
            first = jnp.min(jnp.where(work == mx, j_f, float(n_blocks)), axis=0, keepdims=True)
            work = jnp.where(j_f == first, -jnp.inf, work)
        picked_mask = jnp.where(candidates > NEG / 2, jnp.where(work == -jnp.inf, 0.0, MASKV), MASKV)
        sel_mask = jnp.where(forced, 0.0, picked_mask).astype(BF16)
        qa_ref[g, 0:128, :] = jnp.concatenate([sel_mask] * NSA_HPG, axis=1)
        qa_ref[g, 128:256, :] = q_both

        q_f = q_g.astype(F32)
        for m_ref, acc_ref, own_keys in ((ms_ref, accs_ref, own_ks_t), (mw_ref, accw_ref, own_kw_t)):
            own = jnp.concatenate([own_keys[gs]] * NSA_HPG, axis=1)
            m_ref[g] = jnp.sum(q_f * own, axis=0, keepdims=True)
            acc_ref[g] = jnp.zeros((VT_ROWS, ROWS), F32)

    groups = range(NSA_KV_HEADS)

    def sel_chains(cs, biases=None):
        keys = [ka_ref[chunk_rows(c), :] for c in cs]
        biases = biases or [None] * len(cs)
        return [([(k, qa_ref[g], bias if bias is None else bias[g], vst_ref[c, g])
                  for k, c, bias in zip(keys, cs, biases)], ms_ref.at[g], accs_ref.at[g]) for g in groups]

    def win_chains(cs, biases):
        keys = [kw_ref[0, chunk_rows(c), :] for c in cs]
        return [([(k, qa_ref[g, 128:256, :], bias[g], vwt_ref[c, g])
                  for k, c, bias in zip(keys, cs, biases)], mw_ref.at[g], accw_ref.at[g]) for g in groups]

    @pl.when(qt >= 2)
    def _():
        w2_mask = jnp.where(q_loc < lax.broadcasted_iota(jnp.int32, (CK, ROWS), 0), 0.0, MASKV)
        ta, tb = [ta_ref[g] for g in groups], [tb_ref[g] for g in groups]
        _attend_from_reference(sel_chains([qt, qt - 1], [tb, ta])
                               + win_chains([qt, qt - 1, qt - 2], [tb, ta, [w2_mask] * NSA_KV_HEADS]))

    @pl.when(qt == 1)
    def _():
        ta, tb = [ta_ref[g] for g in groups], [tb_ref[g] for g in groups]
        _attend_from_reference(sel_chains([1, 0], [tb, ta]) + win_chains([1, 0], [tb, ta]))

    @pl.when(qt == 0)
    def _():
        tb = [tb_ref[g] for g in groups]
        _attend_from_reference(sel_chains([0], [tb]) + win_chains([0], [tb]))

    n_far = jnp.maximum(qt - 1, 0)

    def far_block(i, carry):
        _attend_from_reference(sel_chains([FAR_BLOCK * i + k for k in range(FAR_BLOCK)]))
        return carry
    lax.fori_loop(0, n_far // FAR_BLOCK, far_block, 0)
    done = (n_far // FAR_BLOCK) * FAR_BLOCK
    size = FAR_BLOCK // 2
    while size:
        @pl.when((n_far & size) != 0)
        def _(done=done, size=size):
            _attend_from_reference(sel_chains([done + k for k in range(size)]))
        done = done + (n_far & size)
        size //= 2

    for g in range(NSA_KV_HEADS):
        o_s = accs_ref[g, 0:HEAD_DIM, :] / accs_ref[g, HEAD_DIM:HEAD_DIM + 1, :]
        o_w = accw_ref[g, 0:HEAD_DIM, :] / accw_ref[g, HEAD_DIM:HEAD_DIM + 1, :]

        def gate(branch):
            return jnp.concatenate(
                [gates_t[(g * NSA_HPG + hl) * 3 + branch:(g * NSA_HPG + hl) * 3 + branch + 1, :]
                 for hl in range(NSA_HPG)], axis=1)
        mixed = gate(0) * oc_ref[g] + gate(1) * o_s + gate(2) * o_w
        for hl in range(NSA_HPG):
            hd = g * NSA_HPG + hl
            outt_ref[hd * HEAD_DIM:(hd + 1) * HEAD_DIM, :] = mixed[:, hl * TQ:(hl + 1) * TQ]

    out_ref[0] = outt_ref[...].T


def _attention(q, gates, kc, vc, kv, tabs):
    b, s, _ = q.shape
    n_blocks = s // CMP_BLOCK
    nch = s // CK
    ta, tb, d1, d2 = tabs
    tile = lambda w: pl.BlockSpec((1, TQ, w), lambda bi, qi: (bi, qi, 0))
    per_b = lambda r, w, col: pl.BlockSpec((1, r, w), lambda bi, qi: (bi, 0, col), pipeline_mode=pl.Buffered(1))
    return pl.pallas_call(
        _attn_kernel,
        grid=(b, s // TQ),
        in_specs=[tile(NSA_WIDTH), tile(128),
                  per_b(n_blocks, KV_WIDTH, 0), per_b(n_blocks, KV_WIDTH, 0),
                  per_b(s, KV_WIDTH, 0), per_b(s, KV_WIDTH, 1), per_b(s, KV_WIDTH, 2), per_b(s, KV_WIDTH, 3),
                  _const_spec(ta.shape), _const_spec(tb.shape), _const_spec(d1.shape), _const_spec(d2.shape)],
        out_specs=tile(NSA_WIDTH),
        out_shape=jax.ShapeDtypeStruct((b, s, NSA_WIDTH), F32),
        scratch_shapes=[pltpu.VMEM((s, CK), BF16),
                        pltpu.VMEM((nch, NSA_KV_HEADS, VT_ROWS, CK), BF16),
                        pltpu.VMEM((nch, NSA_KV_HEADS, VT_ROWS, CK), BF16),
                        pltpu.VMEM((KV_WIDTH, n_blocks), BF16),
                        pltpu.VMEM((NSA_KV_HEADS, CK, ROWS), BF16),
                        pltpu.VMEM((NSA_KV_HEADS, HEAD_DIM, ROWS), F32),
                        pltpu.VMEM((NSA_KV_HEADS, 1, ROWS), F32), pltpu.VMEM((NSA_KV_HEADS, VT_ROWS, ROWS), F32),
                        pltpu.VMEM((NSA_KV_HEADS, 1, ROWS), F32), pltpu.VMEM((NSA_KV_HEADS, VT_ROWS, ROWS), F32),
                        pltpu.VMEM((NSA_WIDTH, TQ), F32)],
        compiler_params=_params(("parallel", "arbitrary")),
        name="nsa_attention",
    )(q, gates, kc, vc, kv, kv, kv, kv, ta, tb, d1, d2)


def _lru_kernel(xgr_ref, cw_ref, cb_ref, wa_ref, ba_ref, wx_ref, bx_ref, lam_ref,
                out_ref, xbuf_ref, h_ref):
    t = pl.program_id(1)
    ts = LRU_TILE
    w = out_ref.shape[2]

    @pl.when(t == 0)
    def _():
        xbuf_ref[0:8, :] = jnp.zeros((8, w), F32)
        h_ref[...] = jnp.zeros_like(h_ref)

    xg = xgr_ref[0, :, :w]
    xbuf_ref[8:, :] = xgr_ref[0, :, w:]
    xc = cb_ref[...] + xbuf_ref[8:, :] * cw_ref[CONV_WIDTH - 1:CONV_WIDTH, :]
    for k in range(1, CONV_WIDTH):
        xc = xc + xbuf_ref[8 - k:8 - k + ts, :] * cw_ref[CONV_WIDTH - 1 - k:CONV_WIDTH - k, :]
    xbuf_ref[0:8, :] = xbuf_ref[ts:ts + 8, :]

    xcb = xc.astype(BF16)
    r = jax.nn.sigmoid(_dot(xcb, wa_ref[...]) + ba_ref[...])
    i = jax.nn.sigmoid(_dot(xcb, wx_ref[...]) + bx_ref[...])
    nl = -lam_ref[...]
    softplus = jnp.maximum(nl, 0.0) + jnp.log(1.0 + jnp.exp(-jnp.abs(nl)))
    log_a = (-LRU_C * softplus) * r
    a = jnp.exp(log_a)
    bv = jnp.sqrt(1.0 - a * a) * (i * xc)

    row8 = lax.broadcasted_iota(jnp.int32, (8, w), 0)
    y = jax.nn.gelu(xg)
    hc = h_ref[0:1, :]
    for grp in range(ts // 8):
        rs = slice(8 * grp, 8 * grp + 8)
        ag, bg = a[rs], bv[rs]
        sh = 1
        while sh < 8:
            a_prev = jnp.where(row8 >= sh, pltpu.roll(ag, sh, 0), 1.0)
            b_prev = jnp.where(row8 >= sh, pltpu.roll(bg, sh, 0), 0.0)
            bg = ag * b_prev + bg
            ag = ag * a_prev
            sh *= 2
        hg = bg + ag * hc
        out_ref[0, rs, :] = hg * y[rs]
        hc = hg[7:8]
    h_ref[0:1, :] = hc


def _lru(xgr, cw, cb, wa, ba, wx, bx, lam):
    b, s, w2 = xgr.shape
    w = w2 // 2
    ts = LRU_TILE
    return pl.pallas_call(
        _lru_kernel,
        grid=(b, s // ts),
        in_specs=[pl.BlockSpec((1, ts, w2), lambda bi, ti: (bi, ti, 0)),
                  _const_spec(cw.shape), _const_spec(cb.shape), _const_spec(wa.shape), _const_spec(ba.shape),
                  _const_spec(wx.shape), _const_spec(bx.shape), _const_spec(lam.shape)],
        out_specs=pl.BlockSpec((1, ts, w), lambda bi, ti: (bi, ti, 0)),
        out_shape=jax.ShapeDtypeStruct((b, s, w), F32),
        scratch_shapes=[pltpu.VMEM((ts + 8, w), F32), pltpu.VMEM((8, w), F32)],
        compiler_params=_params(("parallel", "arbitrary")),
        name="rg_lru",
    )(xgr, cw, cb, wa, ba, wx, bx, lam)


def _post_kernel(x_ref, att_ref, lru_ref, kt_ref, v_ref,
                 ga_ref, gl_ref, wa_ref, wl_ref, gmix_ref,
                 gxpre_ref, wq_ref, wo_ref, gxpost_ref,
                 gmpre_ref, w1_ref, w2_ref, gmpost_ref, out_ref):
    tm, d = x_ref.shape[1], x_ref.shape[2]
    parts = [slice(k * tm // ROW_SPLIT, (k + 1) * tm // ROW_SPLIT) for k in range(ROW_SPLIT)]
    xs = [x_ref[0, r, :] for r in parts]
    a = [_rms(att_ref[0, r, :], ga_ref[...]).astype(BF16) for r in parts]
    l = [_rms(lru_ref[0, r, :], gl_ref[...]).astype(BF16) for r in parts]
    mixed = [_dot(ak, wa_ref[...]) + _dot(lk, wl_ref[...]) for ak, lk in zip(a, l)]
    xs = [x + _rms(mk, gmix_ref[...]) for x, mk in zip(xs, mixed)]

    dh = d // X_HEADS
    hq = [_rms(x, gxpre_ref[...]).astype(BF16) for x in xs]
    cq = [(_dot(h, wq_ref[...]) * (dh ** -0.5)).astype(BF16) for h in hq]
    outs = [[] for _ in parts]
    for hh in range(X_HEADS):
        hs = slice(hh * dh, (hh + 1) * dh)
        scores = [_dot(c[:, hs], kt_ref[0, hs, :]) for c in cq]
        for out, s in zip(outs, scores):
            e = jnp.exp(s - jnp.max(s, axis=1, keepdims=True))
            p = e / jnp.sum(e, axis=1, keepdims=True)
            out.append(_dot(p.astype(BF16), v_ref[0, :, hs]).astype(BF16))
    co = [_dot(jnp.concatenate(out, axis=1), wo_ref[...]) for out in outs]
    xs = [x + _rms(c, gxpost_ref[...]) for x, c in zip(xs, co)]

    hm = [_rms(x, gmpre_ref[...]).astype(BF16) for x in xs]
    accs = [jnp.zeros(x.shape, F32) for x in xs]
    for c in range(w1_ref.shape[1] // FF_CHUNK):
        cs = slice(c * FF_CHUNK, (c + 1) * FF_CHUNK)
        us = [jnp.maximum(_dot(h, w1_ref[:, cs]), 0.0) for h in hm]
        accs = [acc + _dot((u * u).astype(BF16), w2_ref[cs, :]) for acc, u in zip(accs, us)]
    for r, x, acc in zip(parts, xs, accs):
        out_ref[0, r, :] = x + _rms(acc, gmpost_ref[...])


def _post(x, att, lru, kt, v, gains_and_weights):
    b, s, d = x.shape
    m = v.shape[1]
    tm = ROW_TILE
    row = lambda w: pl.BlockSpec((1, tm, w), lambda bi, i: (bi, i, 0))
    resident = lambda a: pl.BlockSpec(a.shape, lambda *_: (0,) * a.ndim, pipeline_mode=pl.Buffered(1))
    return pl.pallas_call(
        _post_kernel,
        grid=(b, s // tm),
        in_specs=[row(d), row(att.shape[2]), row(lru.shape[2]),
                  pl.BlockSpec((1, d, m), lambda bi, i: (bi, 0, 0)),
                  pl.BlockSpec((1, m, d), lambda bi, i: (bi, 0, 0))]
                 + [resident(a) for a in gains_and_weights],
        out_specs=row(d),
        out_shape=jax.ShapeDtypeStruct((b, s, d), F32),
        compiler_params=_params(("parallel", "parallel")),
        name="post_mixer",
    )(x, att, lru, kt, v, *gains_and_weights)


def _memkv_kernel(mem_ref, g_ref, wk_ref, wv_ref, kt_ref, v_ref):
    mn = _rms(mem_ref[0], g_ref[...]).astype(BF16)
    kt_ref[0] = _dot(mn, wk_ref[...]).T.astype(BF16)
    v_ref[0] = _dot(mn, wv_ref[...]).astype(BF16)


def _memkv(mem, g, wk, wv):
    b, m, d = mem.shape
    return pl.pallas_call(
        _memkv_kernel,
        grid=(b,),
        in_specs=[pl.BlockSpec((1, m, d), lambda i: (i, 0, 0)), _const_spec(g.shape),
                  _const_spec(wk.shape), _const_spec(wv.shape)],
        out_specs=[pl.BlockSpec((1, d, m), lambda i: (i, 0, 0)), pl.BlockSpec((1, m, d), lambda i: (i, 0, 0))],
        out_shape=[jax.ShapeDtypeStruct((b, d, m), BF16), jax.ShapeDtypeStruct((b, m, d), BF16)],
        compiler_params=_params(("parallel",)),
        name="mem_kv",
    )(mem, g, wk, wv)


def _block_structured(w):
    eye = jnp.eye(NSA_KV_HEADS, dtype=w.dtype)
    big = w[:, None, :, None, :] * eye[None, :, None, :, None]
    return big.reshape(CMP_BLOCK * KV_WIDTH, KV_WIDTH).astype(BF16)


def _block_diag(w):
    n, d, e = w.shape
    eye = jnp.eye(n, dtype=w.dtype)
    return (w[:, :, None, :] * eye[:, None, :, None]).reshape(n * d, n * e).astype(BF16)


def kernel(x, mem, rel_bias, ln_mix_pre, ln_mix_post, w_in, cmp_pe_k, cmp_pe_v, cmp_w_k, cmp_w_v,
           conv_w, conv_b, lru_wa, lru_ba, lru_wx, lru_bx, lru_lambda, gn_attn, gn_lru, w_out,
           ln_x_pre, ln_x_post, ln_mem, xq, xkv, xo, ln_mlp_pre, ln_mlp_post, mlp_w1, mlp_w2):
    b, s, d = x.shape
    depth = w_in.shape[0]
    n = b * s
    lru_w = conv_w.shape[2]
    assert s % TQ == 0 and s % LRU_TILE == 0 and n % ROW_TILE == 0
    assert s // CMP_BLOCK == 128, "selection mask layout assumes 128 compression blocks"
    row1 = lambda v: v.reshape(1, -1)

    tabs = _bias_tables(rel_bias)
    x2 = x.reshape(n, d)
    o_q = NSA_WIDTH
    o_g = o_q + 6 * KV_WIDTH
    o_x = o_g + 3 * NSA_HEADS
    for l in range(depth):
        wl = w_in[l]
        wq = wl[:, :o_q].astype(BF16)
        wc = wl[:, o_q:o_q + 2 * KV_WIDTH].astype(BF16)
        wkv = wl[:, o_q + 2 * KV_WIDTH:o_g].astype(BF16)
        wg = jnp.pad(wl[:, o_g:o_x], ((0, 0), (0, 128 - 3 * NSA_HEADS))).astype(BF16)
        wx = wl[:, o_x:].astype(BF16)
        q, kcr, vcr, kv, gt, xgr = _inproj(x2, row1(ln_mix_pre[l]), wq, wc, wkv, wg, wx)

        nblk = n // CMP_BLOCK
        pek = jnp.tile(cmp_pe_k[l], (1, NSA_KV_HEADS)).reshape(1, -1)
        pev = jnp.tile(cmp_pe_v[l], (1, NSA_KV_HEADS)).reshape(1, -1)
        kc, vc = _compress(kcr.reshape(nblk, CMP_BLOCK * KV_WIDTH), vcr.reshape(nblk, CMP_BLOCK * KV_WIDTH),
                           pek, pev, _block_structured(cmp_w_k[l]), _block_structured(cmp_w_v[l]))

        att = _attention(q.reshape(b, s, NSA_WIDTH), gt.reshape(b, s, 128),
                         kc.reshape(b, s // CMP_BLOCK, KV_WIDTH), vc.reshape(b, s // CMP_BLOCK, KV_WIDTH),
                         kv.reshape(b, s, 4 * KV_WIDTH), tabs)
        lru = _lru(xgr.reshape(b, s, 2 * lru_w), conv_w[l], row1(conv_b[l]),
                   _block_diag(lru_wa[l]), row1(lru_ba[l]), _block_diag(lru_wx[l]), row1(lru_bx[l]),
                   row1(lru_lambda[l]))

        wo = w_out[l].astype(BF16)
        wkv_x = xkv[l].astype(BF16)
        kt, v = _memkv(mem, row1(ln_mem[l]), wkv_x[:, :d], wkv_x[:, d:])
        x2 = _post(x2.reshape(b, s, d), att, lru, kt, v,
                   (row1(gn_attn[l]), row1(gn_lru[l]), wo[:NSA_WIDTH], wo[NSA_WIDTH:], row1(ln_mix_post[l]),
                    row1(ln_x_pre[l]), xq[l].astype(BF16), xo[l].astype(BF16), row1(ln_x_post[l]),
                    row1(ln_mlp_pre[l]), mlp_w1[l].astype(BF16), mlp_w2[l].astype(BF16),
                    row1(ln_mlp_post[l]))).reshape(n, d)
    return x2.reshape(b, s, d)
```

```python
import functools
import math

import numpy as np
import jax
import jax.numpy as jnp
from jax import lax
from jax.experimental import pallas as pl
from jax.experimental.pallas import tpu as pltpu

HEAD_DIM = 64
NSA_HEADS = 8
NSA_KV_HEADS = 2
NSA_HPG = NSA_HEADS // NSA_KV_HEADS
NSA_WIDTH = NSA_HEADS * HEAD_DIM
KV_WIDTH = NSA_KV_HEADS * HEAD_DIM
CMP_BLOCK = 64
SEL_TOPN = 16
WINDOW = 512
FORCE_SCORE = 1e4
LRU_BLOCKS = 8
CONV_WIDTH = 4
LRU_C = 8.0
X_HEADS = 4
N_BUCKETS = 32
MAX_DISTANCE = 128
EPS = 1e-6
NEG = -1e30

TQ = 256
CK = 256
ROWS = NSA_HPG * TQ
VT_ROWS = HEAD_DIM + 16
LOG2E = 1.4426950408889634
FAR_BLOCK = 8
PIPE_DEPTH = 4
MASKV = -(2.0 ** 100)
MAX_RISE = 64.0
ROW_TILE = 512
FF_CHUNK = 1024
SELECT_SPANS = 4
ROW_SPLIT = 2
LRU_TILE = 512
VMEM_LIMIT = 56 * 1024 * 1024

F32 = jnp.float32
BF16 = jnp.bfloat16


def _bucket_of_distance(d):
    max_exact = N_BUCKETS // 2
    d = np.maximum(d, 0)
    df = np.maximum(d, 1).astype(np.float64)
    large = max_exact + (np.log(df / max_exact) / math.log(MAX_DISTANCE / max_exact)
                         * (N_BUCKETS - max_exact)).astype(np.int32)
    large = np.minimum(large, N_BUCKETS - 1)
    return np.where(d < max_exact, d, large).astype(np.int32)


def _rms(x, g):
    return x * lax.rsqrt(jnp.mean(x * x, axis=-1, keepdims=True) + EPS) * g


def _dot(a, b):
    return jnp.dot(a, b, preferred_element_type=F32)


def _params(sem):
    return pltpu.CompilerParams(dimension_semantics=sem, vmem_limit_bytes=VMEM_LIMIT)


def _const_spec(shape):
    nd = len(shape)
    return pl.BlockSpec(shape, lambda *_: (0,) * nd)


def _inproj_kernel(x_ref, g_ref, wq_ref, wc_ref, wkv_ref, wg_ref, wx_ref,
                   q_ref, kcr_ref, vcr_ref, kv_ref, gt_ref, xgr_ref):
    h = _rms(x_ref[...], g_ref[...]).astype(BF16)
    q_ref[...] = (_dot(h, wq_ref[...]) * (HEAD_DIM ** -0.5 * LOG2E)).astype(BF16)
    c = _dot(h, wc_ref[...])
    kcr_ref[...] = c[:, :KV_WIDTH]
    vcr_ref[...] = c[:, KV_WIDTH:]
    kv_ref[...] = _dot(h, wkv_ref[...]).astype(BF16)
    gt_ref[...] = _dot(h, wg_ref[...])
    xgr_ref[...] = _dot(h, wx_ref[...])


def _inproj(x2, gain, wq, wc, wkv, wg, wx):
    n, d = x2.shape
    tm = 2 * ROW_TILE
    row = lambda w: pl.BlockSpec((tm, w), lambda i: (i, 0))
    return pl.pallas_call(
        _inproj_kernel,
        grid=(n // tm,),
        in_specs=[row(d), _const_spec(gain.shape), _const_spec(wq.shape), _const_spec(wc.shape),
                  _const_spec(wkv.shape), _const_spec(wg.shape), _const_spec(wx.shape)],
        out_specs=[row(NSA_WIDTH), row(KV_WIDTH), row(KV_WIDTH), row(4 * KV_WIDTH), row(128),
                   row(wx.shape[1])],
        out_shape=[jax.ShapeDtypeStruct((n, NSA_WIDTH), BF16),
                   jax.ShapeDtypeStruct((n, KV_WIDTH), F32),
                   jax.ShapeDtypeStruct((n, KV_WIDTH), F32),
                   jax.ShapeDtypeStruct((n, 4 * KV_WIDTH), BF16),
                   jax.ShapeDtypeStruct((n, 128), F32),
                   jax.ShapeDtypeStruct((n, wx.shape[1]), F32)],
        compiler_params=_params(("parallel",)),
        name="inproj",
    )(x2, gain, wq, wc, wkv, wg, wx)


def _compress_kernel(kr_ref, vr_ref, pek_ref, pev_ref, wk_ref, wv_ref, kc_ref, vc_ref):
    kc_ref[...] = _dot((kr_ref[...] + pek_ref[...]).astype(BF16), wk_ref[...]).astype(BF16)
    vc_ref[...] = _dot((vr_ref[...] + pev_ref[...]).astype(BF16), wv_ref[...]).astype(BF16)


def _compress(kr, vr, pek, pev, wk, wv):
    nblk, width = kr.shape
    tm = 128
    row = lambda w: pl.BlockSpec((tm, w), lambda i: (i, 0))
    return pl.pallas_call(
        _compress_kernel,
        grid=(nblk // tm,),
        in_specs=[row(width), row(width), _const_spec(pek.shape), _const_spec(pev.shape),
                  _const_spec(wk.shape), _const_spec(wv.shape)],
        out_specs=[row(KV_WIDTH), row(KV_WIDTH)],
        out_shape=[jax.ShapeDtypeStruct((nblk, KV_WIDTH), BF16)] * 2,
        compiler_params=_params(("parallel",)),
        name="compress",
    )(kr, vr, pek, pev, wk, wv)


def _bias_tables_kernel(rb_ref, bma_ref, bmb_ref, bm1_ref, bm2_ref,
                        ta_ref, tb_ref, d1_ref, d2_ref):
    last = N_BUCKETS - 1
    for hd in range(NSA_HEADS):
        g, hl = divmod(hd, NSA_HPG)
        cols = slice(hl * TQ, (hl + 1) * TQ)
        far = rb_ref[last, hd]
        for bm_ref, out_ref, masked in ((bma_ref, ta_ref, False), (bmb_ref, tb_ref, True),
                                        (bm1_ref, d1_ref, False), (bm2_ref, d2_ref, False)):
            bm = bm_ref[...]
            acc = jnp.zeros(bm.shape, F32)
            for k in range(N_BUCKETS - 1):
                acc = jnp.where(bm == k, (rb_ref[k, hd] - far) * LOG2E, acc)
            if masked:
                acc = jnp.where(bm < 0, MASKV, acc)
            out_ref[g, :, cols] = acc


def _bias_tables(rel_bias):
    ki = np.arange(CK)[:, None]
    qi = np.arange(TQ)[None, :]
    bma = _bucket_of_distance(qi - ki + CK)
    bmb = np.where(qi >= ki, _bucket_of_distance(qi - ki), -1).astype(np.int32)
    r = (np.arange(TQ) + 1) % CMP_BLOCK
    bm1 = np.broadcast_to(_bucket_of_distance(r)[None, :], (8, TQ)).astype(np.int32)
    bm2 = np.broadcast_to(_bucket_of_distance(r + CMP_BLOCK)[None, :], (8, TQ)).astype(np.int32)
    return pl.pallas_call(
        _bias_tables_kernel,
        in_specs=[pl.BlockSpec(memory_space=pltpu.SMEM)] + [pl.BlockSpec(memory_space=pltpu.VMEM)] * 4,
        out_specs=[pl.BlockSpec(memory_space=pltpu.VMEM)] * 4,
        out_shape=[jax.ShapeDtypeStruct((NSA_KV_HEADS, CK, ROWS), F32),
                   jax.ShapeDtypeStruct((NSA_KV_HEADS, CK, ROWS), F32),
                   jax.ShapeDtypeStruct((NSA_KV_HEADS, 8, ROWS), F32),
                   jax.ShapeDtypeStruct((NSA_KV_HEADS, 8, ROWS), F32)],
        name="bias_tables",
    )(rel_bias, jnp.asarray(bma), jnp.asarray(bmb), jnp.asarray(bm1), jnp.asarray(bm2))


def _attend_pass(chains, exact_items):
    items = [(ci, chunk) for k in range(max(len(c[0]) for c in chains))
             for ci, c in enumerate(chains) for chunk in c[0][k:k + 1]]

    def score(item):
        keys, q, bias, _ = item[1]
        s = _dot(keys, q)
        return s if bias is None else s + bias

    scores = [score(item) for item in items[:PIPE_DEPTH]]
    start = [(m_ref[...], acc_ref[...]) for _, m_ref, acc_ref in chains]
    state = list(start)
    tops = [None] * len(chains)
    seen = [0] * len(chains)
    for i, (ci, chunk) in enumerate(items):
        s = scores[i]
        m, acc = state[ci]
        top = jnp.max(s, axis=0, keepdims=True)
        if exact_items is None or seen[ci] < exact_items:
            m_new = jnp.maximum(m, top)
            p = jnp.exp2(s - m_new).astype(BF16)
            state[ci] = (m_new, jnp.exp2(m - m_new) * acc + _dot(chunk[3], p))
        else:
            tops[ci] = top if tops[ci] is None else jnp.maximum(tops[ci], top)
            state[ci] = (m, acc + _dot(chunk[3], jnp.exp2(s - m).astype(BF16)))
        seen[ci] += 1
        if i + PIPE_DEPTH < len(items):
            scores.append(score(items[i + PIPE_DEPTH]))
    rise = [top - m for top, (m, _) in zip(tops, state) if top is not None]
    moved = []
    for (m, acc), top in zip(state, tops):
        if top is not None:
            m_new = jnp.maximum(m, top)
            m, acc = m_new, jnp.exp2(m - m_new) * acc
        moved.append((m, acc))
    return moved, rise


def _commit(chains, state):
    for (_, m_ref, acc_ref), (m, acc) in zip(chains, state):
        m_ref[...] = m
        acc_ref[...] = acc


def _attend(chains):
    _commit(chains, _attend_pass(chains, None)[0])


def _attend_from_reference(chains, exact_items=0):
    state, rise = _attend_pass(chains, exact_items)
    worst = rise[0]
    for r in rise[1:]:
        worst = jnp.maximum(worst, r)
    safe = jnp.max(worst) <= MAX_RISE

    @pl.when(safe)
    def _():
        _commit(chains, state)

    @pl.when(jnp.logical_not(safe))
    def _():
        _attend(chains)


def _attn_kernel(q_ref, gt_ref, kc_ref, vc_ref, ks_ref, vs_ref, kw_ref, vw_ref,
                 ta_ref, tb_ref, d1_ref, d2_ref, out_ref,
                 ka_ref, vst_ref, vwt_ref, vct_ref, qa_ref, oc_ref, ms_ref, accs_ref, mw_ref, accw_ref, outt_ref):
    qt = pl.program_id(1)
    n_blocks = kc_ref.shape[1]
    blocks_per_tile = TQ // CMP_BLOCK

    @pl.when(qt == 0)
    def _():
        ones_rows = jnp.where(lax.broadcasted_iota(jnp.int32, (VT_ROWS - HEAD_DIM, CK), 0) == 0,
                              1.0, 0.0).astype(BF16)

        def build(c, carry):
            rs = pl.ds(pl.multiple_of(c * CK, CK), CK)
            blk = lax.broadcasted_iota(jnp.int32, (CK, 128), 1)
            key_blk = c * (CK // CMP_BLOCK) + (lax.broadcasted_iota(jnp.int32, (CK, 128), 0) >> 6)
            ka_ref[rs, 0:128] = jnp.where(blk == key_blk, 1.0, 0.0).astype(BF16)
            ka_ref[rs, 128:256] = ks_ref[0, rs, :]
            vs_t = vs_ref[0, rs, :].astype(F32).T
            vw_t = vw_ref[0, rs, :].astype(F32).T
            for g in range(NSA_KV_HEADS):
                gs = slice(g * HEAD_DIM, (g + 1) * HEAD_DIM)
                vst_ref[c, g, 0:HEAD_DIM, :] = vs_t[gs].astype(BF16)
                vst_ref[c, g, HEAD_DIM:, :] = ones_rows
                vwt_ref[c, g, 0:HEAD_DIM, :] = vw_t[gs].astype(BF16)
                vwt_ref[c, g, HEAD_DIM:, :] = ones_rows
            return carry
        lax.fori_loop(0, ks_ref.shape[1] // CK, build, 0)
        vct_ref[...] = vc_ref[0].astype(F32).T.astype(BF16)

    q_t = q_ref[0].astype(F32).T
    gates_t = jax.nn.sigmoid(gt_ref[0]).T
    kc = kc_ref[0]
    own_rows = pl.ds(pl.multiple_of(qt * CK, CK), CK)
    own_ks_t = ka_ref[own_rows, 128:256].astype(F32).T
    own_kw_t = kw_ref[0, own_rows, :].astype(F32).T

    q_loc = lax.broadcasted_iota(jnp.int32, (1, ROWS), 1) & (TQ - 1)
    n1 = qt * blocks_per_tile + ((q_loc + 1) >> 6) - 1
    cur = qt * blocks_per_tile + (lax.broadcasted_iota(jnp.int32, (1, TQ), 1) >> 6)
    zeros_half = jnp.zeros((HEAD_DIM, ROWS), BF16)
    groups = range(NSA_KV_HEADS)

    def chunk_rows(c):
        return pl.ds(pl.multiple_of(c * CK, CK), CK)

    for g in groups:
        gs = slice(g * HEAD_DIM, (g + 1) * HEAD_DIM)
        q_g = jnp.concatenate(
            [q_t[(g * NSA_HPG + hl) * HEAD_DIM:(g * NSA_HPG + hl + 1) * HEAD_DIM, :]
             for hl in range(NSA_HPG)], axis=1).astype(BF16)
        qa_ref[g, 128:256, :] = jnp.concatenate([q_g, zeros_half] if g == 0 else [zeros_half, q_g], axis=0)

        q_f = q_g.astype(F32)
        for m_ref, acc_ref, own_keys in ((ms_ref, accs_ref, own_ks_t), (mw_ref, accw_ref, own_kw_t)):
            own = jnp.concatenate([own_keys[gs]] * NSA_HPG, axis=1)
            m_ref[g] = jnp.sum(q_f * own, axis=0, keepdims=True)
            acc_ref[g] = jnp.zeros((VT_ROWS, ROWS), F32)

    def compress_and_select(nb):
        n_iota = lax.broadcasted_iota(jnp.int32, (nb, ROWS), 0)
        j_s = lax.broadcasted_iota(jnp.int32, (nb, TQ), 0)
        j_f = j_s.astype(F32)
        for g in groups:
            gs = slice(g * HEAD_DIM, (g + 1) * HEAD_DIM)
            s_c = _dot(kc[:nb], qa_ref[g, 128:256, :])
            bias_c = (jnp.where(n_iota == n1, d1_ref[g, 0:1, :], 0.0)
                      + jnp.where(n_iota == n1 - 1, d2_ref[g, 0:1, :], 0.0))
            valid_c = n_iota <= n1
            l_c = jnp.where(valid_c, s_c + bias_c, NEG)
            m_c = jnp.max(l_c, axis=0, keepdims=True)
            e_c = jnp.where(valid_c, jnp.exp2(l_c - m_c), 0.0)
            p_c = e_c / jnp.maximum(jnp.sum(e_c, axis=0, keepdims=True), 1e-30)
            p_all = p_c.astype(BF16)
            if nb < n_blocks:
                p_all = jnp.concatenate([p_all, jnp.zeros((n_blocks - nb, ROWS), BF16)], axis=0)
            oc_ref[g] = _dot(vct_ref[gs, :], p_all)

            imp = p_c[:, 0:TQ]
            for hl in range(1, NSA_HPG):
                imp = imp + p_c[:, hl * TQ:(hl + 1) * TQ]
            forced = (j_s == 0) | (j_s == cur) | (j_s == cur - 1)
            candidates = jnp.where(forced, NEG, jnp.where(j_s <= cur, imp, NEG))
            work = candidates
            for _ in range(SEL_TOPN - 3):
                mx = jnp.max(work, axis=0, keepdims=True)
                first = jnp.min(jnp.where(work == mx, j_f, float(nb)), axis=0, keepdims=True)
                work = jnp.where(j_f == first, -jnp.inf, work)
            picked_mask = jnp.where(candidates > NEG / 2, jnp.where(work == -jnp.inf, 0.0, MASKV), MASKV)
            sel_mask = jnp.where(forced, 0.0, picked_mask).astype(BF16)
            qa_ref[g, 0:nb, :] = jnp.concatenate([sel_mask] * NSA_HPG, axis=1)
            if nb < n_blocks:
                qa_ref[g, nb:n_blocks, :] = jnp.full((n_blocks - nb, ROWS), MASKV, BF16)

    tiles_per_span = n_blocks // SELECT_SPANS // blocks_per_tile
    for k in range(SELECT_SPANS):
        @pl.when(qt // tiles_per_span == k)
        def _(k=k):
            compress_and_select((k + 1) * n_blocks // SELECT_SPANS)

    def sel_chains(cs, biases=None):
        keys = [ka_ref[chunk_rows(c), :] for c in cs]
        biases = biases or [None] * len(cs)
        return [([(k, qa_ref[g], bias if bias is None else bias[g], vst_ref[c, g])
                  for k, c, bias in zip(keys, cs, biases)], ms_ref.at[g], accs_ref.at[g]) for g in groups]

    def win_chains(cs, biases):
        keys = [kw_ref[0, chunk_rows(c), :] for c in cs]
        return [([(k, qa_ref[g, 128:256, :], bias[g], vwt_ref[c, g])
                  for k, c, bias in zip(keys, cs, biases)], mw_ref.at[g], accw_ref.at[g]) for g in groups]

    @pl.when(qt >= 2)
    def _():
        w2_mask = jnp.where(q_loc < lax.broadcasted_iota(jnp.int32, (CK, ROWS), 0), 0.0, MASKV)
        ta, tb = [ta_ref[g] for g in groups], [tb_ref[g] for g in groups]
        _attend_from_reference(sel_chains([qt, qt - 1], [tb, ta])
                               + win_chains([qt, qt - 1, qt - 2], [tb, ta, [w2_mask] * NSA_KV_HEADS]))

    @pl.when(qt == 1)
    def _():
        ta, tb = [ta_ref[g] for g in groups], [tb_ref[g] for g in groups]
        _attend_from_reference(sel_chains([1, 0], [tb, ta]) + win_chains([1, 0], [tb, ta]))

    @pl.when(qt == 0)
    def _():
        tb = [tb_ref[g] for g in groups]
        _attend_from_reference(sel_chains([0], [tb]) + win_chains([0], [tb]))

    n_far = jnp.maximum(qt - 1, 0)

    def far_block(i, carry):
        _attend_from_reference(sel_chains([FAR_BLOCK * i + k for k in range(FAR_BLOCK)]))
        return carry
    lax.fori_loop(0, n_far // FAR_BLOCK, far_block, 0)
    done = (n_far // FAR_BLOCK) * FAR_BLOCK
    size = FAR_BLOCK // 2
    while size:
        @pl.when((n_far & size) != 0)
        def _(done=done, size=size):
            _attend_from_reference(sel_chains([done + k for k in range(size)]))
        done = done + (n_far & size)
        size //= 2

    for g in range(NSA_KV_HEADS):
        o_s = accs_ref[g, 0:HEAD_DIM, :] / accs_ref[g, HEAD_DIM:HEAD_DIM + 1, :]
        o_w = accw_ref[g, 0:HEAD_DIM, :] / accw_ref[g, HEAD_DIM:HEAD_DIM + 1, :]

        def gate(branch):
            return jnp.concatenate(
                [gates_t[(g * NSA_HPG + hl) * 3 + branch:(g * NSA_HPG + hl) * 3 + branch + 1, :]
                 for hl in range(NSA_HPG)], axis=1)
        mixed = gate(0) * oc_ref[g] + gate(1) * o_s + gate(2) * o_w
        for hl in range(NSA_HPG):
            hd = g * NSA_HPG + hl
            outt_ref[hd * HEAD_DIM:(hd + 1) * HEAD_DIM, :] = mixed[:, hl * TQ:(hl + 1) * TQ]

    out_ref[0] = outt_ref[...].T


def _attention(q, gates, kc, vc, kv, tabs):
    b, s, _ = q.shape
    n_blocks = s // CMP_BLOCK
    nch = s // CK
    ta, tb, d1, d2 = tabs
    tile = lambda w: pl.BlockSpec((1, TQ, w), lambda bi, qi: (bi, qi, 0))
    per_b = lambda r, w, col: pl.BlockSpec((1, r, w), lambda bi, qi: (bi, 0, col), pipeline_mode=pl.Buffered(1))
    return pl.pallas_call(
        _attn_kernel,
        grid=(b, s // TQ),
        in_specs=[tile(NSA_WIDTH), tile(128),
                  per_b(n_blocks, KV_WIDTH, 0), per_b(n_blocks, KV_WIDTH, 0),
                  per_b(s, KV_WIDTH, 0), per_b(s, KV_WIDTH, 1), per_b(s, KV_WIDTH, 2), per_b(s, KV_WIDTH, 3),
                  _const_spec(ta.shape), _const_spec(tb.shape), _const_spec(d1.shape), _const_spec(d2.shape)],
        out_specs=tile(NSA_WIDTH),
        out_shape=jax.ShapeDtypeStruct((b, s, NSA_WIDTH), F32),
        scratch_shapes=[pltpu.VMEM((s, CK), BF16),
                        pltpu.VMEM((nch, NSA_KV_HEADS, VT_ROWS, CK), BF16),
                        pltpu.VMEM((nch, NSA_KV_HEADS, VT_ROWS, CK), BF16),
                        pltpu.VMEM((KV_WIDTH, n_blocks), BF16),
                        pltpu.VMEM((NSA_KV_HEADS, CK, ROWS), BF16),
                        pltpu.VMEM((NSA_KV_HEADS, HEAD_DIM, ROWS), F32),
                        pltpu.VMEM((NSA_KV_HEADS, 1, ROWS), F32), pltpu.VMEM((NSA_KV_HEADS, VT_ROWS, ROWS), F32),
                        pltpu.VMEM((NSA_KV_HEADS, 1, ROWS), F32), pltpu.VMEM((NSA_KV_HEADS, VT_ROWS, ROWS), F32),
                        pltpu.VMEM((NSA_WIDTH, TQ), F32)],
        compiler_params=_params(("parallel", "arbitrary")),
        name="nsa_attention",
    )(q, gates, kc, vc, kv, kv, kv, kv, ta, tb, d1, d2)


def _lru_kernel(xgr_ref, cw_ref, cb_ref, wa_ref, ba_ref, wx_ref, bx_ref, lam_ref,
                out_ref, xbuf_ref, h_ref):
    t = pl.program_id(1)
    ts = LRU_TILE
    w = out_ref.shape[2]

    @pl.when(t == 0)
    def _():
        xbuf_ref[0:8, :] = jnp.zeros((8, w), F32)
        h_ref[...] = jnp.zeros_like(h_ref)

    xg = xgr_ref[0, :, :w]
    xbuf_ref[8:, :] = xgr_ref[0, :, w:]
    xc = cb_ref[...] + xbuf_ref[8:, :] * cw_ref[CONV_WIDTH - 1:CONV_WIDTH, :]
    for k in range(1, CONV_WIDTH):
        xc = xc + xbuf_ref[8 - k:8 - k + ts, :] * cw_ref[CONV_WIDTH - 1 - k:CONV_WIDTH - k, :]
    xbuf_ref[0:8, :] = xbuf_ref[ts:ts + 8, :]

    xcb = xc.astype(BF16)
    r = jax.nn.sigmoid(_dot(xcb, wa_ref[...]) + ba_ref[...])
    i = jax.nn.sigmoid(_dot(xcb, wx_ref[...]) + bx_ref[...])
    nl = -lam_ref[...]
    softplus = jnp.maximum(nl, 0.0) + jnp.log(1.0 + jnp.exp(-jnp.abs(nl)))
    log_a = (-LRU_C * softplus) * r
    a = jnp.exp(log_a)
    bv = jnp.sqrt(1.0 - a * a) * (i * xc)

    row8 = lax.broadcasted_iota(jnp.int32, (8, w), 0)
    y = jax.nn.gelu(xg)
    hc = h_ref[0:1, :]
    for grp in range(ts // 8):
        rs = slice(8 * grp, 8 * grp + 8)
        ag, bg = a[rs], bv[rs]
        sh = 1
        while sh < 8:
            a_prev = jnp.where(row8 >= sh, pltpu.roll(ag, sh, 0), 1.0)
            b_prev = jnp.where(row8 >= sh, pltpu.roll(bg, sh, 0), 0.0)
            bg = ag * b_prev + bg
            ag = ag * a_prev
            sh *= 2
        hg = bg + ag * hc
        out_ref[0, rs, :] = hg * y[rs]
        hc = hg[7:8]
    h_ref[0:1, :] = hc


def _lru(xgr, cw, cb, wa, ba, wx, bx, lam):
    b, s, w2 = xgr.shape
    w = w2 // 2
    ts = LRU_TILE
    return pl.pallas_call(
        _lru_kernel,
        grid=(b, s // ts),
        in_specs=[pl.BlockSpec((1, ts, w2), lambda bi, ti: (bi, ti, 0)),
                  _const_spec(cw.shape), _const_spec(cb.shape), _const_spec(wa.shape), _const_spec(ba.shape),
                  _const_spec(wx.shape), _const_spec(bx.shape), _const_spec(lam.shape)],
        out_specs=pl.BlockSpec((1, ts, w), lambda bi, ti: (bi, ti, 0)),
        out_shape=jax.ShapeDtypeStruct((b, s, w), F32),
        scratch_shapes=[pltpu.VMEM((ts + 8, w), F32), pltpu.VMEM((8, w), F32)],
        compiler_params=_params(("parallel", "arbitrary")),
        name="rg_lru",
    )(xgr, cw, cb, wa, ba, wx, bx, lam)


def _post_kernel(x_ref, att_ref, lru_ref, kt_ref, v_ref,
                 ga_ref, gl_ref, wa_ref, wl_ref, gmix_ref,
                 gxpre_ref, wq_ref, wo_ref, gxpost_ref,
                 gmpre_ref, w1_ref, w2_ref, gmpost_ref, out_ref):
    tm, d = x_ref.shape[1], x_ref.shape[2]
    parts = [slice(k * tm // ROW_SPLIT, (k + 1) * tm // ROW_SPLIT) for k in range(ROW_SPLIT)]
    xs = [x_ref[0, r, :] for r in parts]
    a = [_rms(att_ref[0, r, :], ga_ref[...]).astype(BF16) for r in parts]
    l = [_rms(lru_ref[0, r, :], gl_ref[...]).astype(BF16) for r in parts]
    mixed = [_dot(ak, wa_ref[...]) + _dot(lk, wl_ref[...]) for ak, lk in zip(a, l)]
    xs = [x + _rms(mk, gmix_ref[...]) for x, mk in zip(xs, mixed)]

    dh = d // X_HEADS
    hq = [_rms(x, gxpre_ref[...]).astype(BF16) for x in xs]
    cq = [(_dot(h, wq_ref[...]) * (dh ** -0.5)).astype(BF16) for h in hq]
    outs = [[] for _ in parts]
    for hh in range(X_HEADS):
        hs = slice(hh * dh, (hh + 1) * dh)
        scores = [_dot(c[:, hs], kt_ref[0, hs, :]) for c in cq]
        for out, s in zip(outs, scores):
            e = jnp.exp(s - jnp.max(s, axis=1, keepdims=True))
            p = e / jnp.sum(e, axis=1, keepdims=True)
            out.append(_dot(p.astype(BF16), v_ref[0, :, hs]).astype(BF16))
    co = [_dot(jnp.concatenate(out, axis=1), wo_ref[...]) for out in outs]
    xs = [x + _rms(c, gxpost_ref[...]) for x, c in zip(xs, co)]

    hm = [_rms(x, gmpre_ref[...]).astype(BF16) for x in xs]
    accs = [jnp.zeros(x.shape, F32) for x in xs]
    for c in range(w1_ref.shape[1] // FF_CHUNK):
        cs = slice(c * FF_CHUNK, (c + 1) * FF_CHUNK)
        us = [jnp.maximum(_dot(h, w1_ref[:, cs]), 0.0) for h in hm]
        accs = [acc + _dot((u * u).astype(BF16), w2_ref[cs, :]) for acc, u in zip(accs, us)]
    for r, x, acc in zip(parts, xs, accs):
        out_ref[0, r, :] = x + _rms(acc, gmpost_ref[...])


def _post(x, att, lru, kt, v, gains_and_weights):
    b, s, d = x.shape
    m = v.shape[1]
    tm = ROW_TILE
    row = lambda w: pl.BlockSpec((1, tm, w), lambda bi, i: (bi, i, 0))
    resident = lambda a: pl.BlockSpec(a.shape, lambda *_: (0,) * a.ndim, pipeline_mode=pl.Buffered(1))
    return pl.pallas_call(
        _post_kernel,
        grid=(b, s // tm),
        in_specs=[row(d), row(att.shape[2]), row(lru.shape[2]),
                  pl.BlockSpec((1, d, m), lambda bi, i: (bi, 0, 0)),
                  pl.BlockSpec((1, m, d), lambda bi, i: (bi, 0, 0))]
                 + [resident(a) for a in gains_and_weights],
        out_specs=row(d),
        out_shape=jax.ShapeDtypeStruct((b, s, d), F32),
        compiler_params=_params(("parallel", "parallel")),
        name="post_mixer",
    )(x, att, lru, kt, v, *gains_and_weights)


def _memkv_kernel(mem_ref, g_ref, wk_ref, wv_ref, kt_ref, v_ref):
    mn = _rms(mem_ref[0], g_ref[...]).astype(BF16)
    kt_ref[0] = _dot(mn, wk_ref[...]).T.astype(BF16)
    v_ref[0] = _dot(mn, wv_ref[...]).astype(BF16)


def _memkv(mem, g, wk, wv):
    b, m, d = mem.shape
    return pl.pallas_call(
        _memkv_kernel,
        grid=(b,),
        in_specs=[pl.BlockSpec((1, m, d), lambda i: (i, 0, 0)), _const_spec(g.shape),
                  _const_spec(wk.shape), _const_spec(wv.shape)],
        out_specs=[pl.BlockSpec((1, d, m), lambda i: (i, 0, 0)), pl.BlockSpec((1, m, d), lambda i: (i, 0, 0))],
        out_shape=[jax.ShapeDtypeStruct((b, d, m), BF16), jax.ShapeDtypeStruct((b, m, d), BF16)],
        compiler_params=_params(("parallel",)),
        name="mem_kv",
    )(mem, g, wk, wv)


def _block_structured(w):
    eye = jnp.eye(NSA_KV_HEADS, dtype=w.dtype)
    big = w[:, None, :, None, :] * eye[None, :, None, :, None]
    return big.reshape(CMP_BLOCK * KV_WIDTH, KV_WIDTH).astype(BF16)


def _block_diag(w):
    n, d, e = w.shape
    eye = jnp.eye(n, dtype=w.dtype)
    return (w[:, :, None, :] * eye[:, None, :, None]).reshape(n * d, n * e).astype(BF16)


def kernel(x, mem, rel_bias, ln_mix_pre, ln_mix_post, w_in, cmp_pe_k, cmp_pe_v, cmp_w_k, cmp_w_v,
           conv_w, conv_b, lru_wa, lru_ba, lru_wx, lru_bx, lru_lambda, gn_attn, gn_lru, w_out,
           ln_x_pre, ln_x_post, ln_mem, xq, xkv, xo, ln_mlp_pre, ln_mlp_post, mlp_w1, mlp_w2):
    b, s, d = x.shape
    depth = w_in.shape[0]
    n = b * s
    lru_w = conv_w.shape[2]
    assert s % TQ == 0 and s % LRU_TILE == 0 and n % ROW_TILE == 0
    assert s // CMP_BLOCK == 128, "selection mask layout assumes 128 compression blocks"
    row1 = lambda v: v.reshape(1, -1)

    tabs = _bias_tables(rel_bias)
    x2 = x.reshape(n, d)
    o_q = NSA_WIDTH
    o_g = o_q + 6 * KV_WIDTH
    o_x = o_g + 3 * NSA_HEADS
    for l in range(depth):
        wl = w_in[l]
        wq = wl[:, :o_q].astype(BF16)
        wc = wl[:, o_q:o_q + 2 * KV_WIDTH].astype(BF16)
        wkv = wl[:, o_q + 2 * KV_WIDTH:o_g].astype(BF16)
        wg = jnp.pad(wl[:, o_g:o_x], ((0, 0), (0, 128 - 3 * NSA_HEADS))).astype(BF16)
        wx = wl[:, o_x:].astype(BF16)
        q, kcr, vcr, kv, gt, xgr = _inproj(x2, row1(ln_mix_pre[l]), wq, wc, wkv, wg, wx)

        nblk = n // CMP_BLOCK
        pek = jnp.tile(cmp_pe_k[l], (1, NSA_KV_HEADS)).reshape(1, -1)
        pev = jnp.tile(cmp_pe_v[l], (1, NSA_KV_HEADS)).reshape(1, -1)
        kc, vc = _compress(kcr.reshape(nblk, CMP_BLOCK * KV_WIDTH), vcr.reshape(nblk, CMP_BLOCK * KV_WIDTH),
                           pek, pev, _block_structured(cmp_w_k[l]), _block_structured(cmp_w_v[l]))

        att = _attention(q.reshape(b, s, NSA_WIDTH), gt.reshape(b, s, 128),
                         kc.reshape(b, s // CMP_BLOCK, KV_WIDTH), vc.reshape(b, s // CMP_BLOCK, KV_WIDTH),
                         kv.reshape(b, s, 4 * KV_WIDTH), tabs)
        lru = _lru(xgr.reshape(b, s, 2 * lru_w), conv_w[l], row1(conv_b[l]),
                   _block_diag(lru_wa[l]), row1(lru_ba[l]), _block_diag(lru_wx[l]), row1(lru_bx[l]),
                   row1(lru_lambda[l]))

        wo = w_out[l].astype(BF16)
        wkv_x = xkv[l].astype(BF16)
        kt, v = _memkv(mem, row1(ln_mem[l]), wkv_x[:, :d], wkv_x[:, d:])
        x2 = _post(x2.reshape(b, s, d), att, lru, kt, v,
                   (row1(gn_attn[l]), row1(gn_lru[l]), wo[:NSA_WIDTH], wo[NSA_WIDTH:], row1(ln_mix_post[l]),
                    row1(ln_x_pre[l]), xq[l].astype(BF16), xo[l].astype(BF16), row1(ln_x_post[l]),
                    row1(ln_mlp_pre[l]), mlp_w1[l].astype(BF16), mlp_w2[l].astype(BF16),
                    row1(ln_mlp_post[l]))).reshape(n, d)
    return x2.reshape(b, s, d)
```

```python
import functools
import math

import numpy as np
import jax
import jax.numpy as jnp
from jax import lax
from jax.experimental import pallas as pl
from jax.experimental.pallas import tpu as pltpu

HEAD_DIM = 64
NSA_HEADS = 8
NSA_KV_HEADS = 2
NSA_HPG = NSA_HEADS // NSA_KV_HEADS
NSA_WIDTH = NSA_HEADS * HEAD_DIM
KV_WIDTH = NSA_KV_HEADS * HEAD_DIM
CMP_BLOCK = 64
SEL_TOPN = 16
WINDOW = 512
FORCE_SCORE = 1e4
LRU_BLOCKS = 8
CONV_WIDTH = 4
LRU_C = 8.0
X_HEADS = 4
N_BUCKETS = 32
MAX_DISTANCE = 128
EPS = 1e-6
NEG = -1e30

TQ = 256
CK = 256
ROWS = NSA_HPG * TQ
VT_ROWS = HEAD_DIM + 16
LOG2E = 1.4426950408889634
FAR_BLOCK = 8
FAR_DEPTH = 4
TAIL_DEPTH = 2
MASKV = -(2.0 ** 100)
MAX_RISE = 64.0
ROW_TILE = 512
FF_CHUNK = 1024
SELECT_SPANS = 8
ROW_SPLIT = 2
LRU_TILE = 1024
VMEM_LIMIT = 56 * 1024 * 1024

F32 = jnp.float32
BF16 = jnp.bfloat16


def _bucket_of_distance(d):
    max_exact = N_BUCKETS // 2
    d = np.maximum(d, 0)
    df = np.maximum(d, 1).astype(np.float64)
    large = max_exact + (np.log(df / max_exact) / math.log(MAX_DISTANCE / max_exact)
                         * (N_BUCKETS - max_exact)).astype(np.int32)
    large = np.minimum(large, N_BUCKETS - 1)
    return np.where(d < max_exact, d, large).astype(np.int32)


def _rms(x, g):
    return x * lax.rsqrt(jnp.mean(x * x, axis=-1, keepdims=True) + EPS) * g


def _dot(a, b):
    return jnp.dot(a, b, preferred_element_type=F32)


def _params(sem):
    return pltpu.CompilerParams(dimension_semantics=sem, vmem_limit_bytes=VMEM_LIMIT)


def _const_spec(shape):
    nd = len(shape)
    return pl.BlockSpec(shape, lambda *_: (0,) * nd)


def _inproj_kernel(x_ref, g_ref, wq_ref, wc_ref, wkv_ref, wg_ref, wx_ref,
                   q_ref, kcr_ref, vcr_ref, kv_ref, gt_ref, xgr_ref):
    h = _rms(x_ref[...], g_ref[...]).astype(BF16)
    q_ref[...] = (_dot(h, wq_ref[...]) * (HEAD_DIM ** -0.5 * LOG2E)).astype(BF16)
    c = _dot(h, wc_ref[...])
    kcr_ref[...] = c[:, :KV_WIDTH]
    vcr_ref[...] = c[:, KV_WIDTH:]
    kv_ref[...] = _dot(h, wkv_ref[...]).astype(BF16)
    gt_ref[...] = _dot(h, wg_ref[...])
    xgr_ref[...] = _dot(h, wx_ref[...])


def _inproj(x2, gain, wq, wc, wkv, wg, wx):
    n, d = x2.shape
    tm = 2 * ROW_TILE
    row = lambda w: pl.BlockSpec((tm, w), lambda i: (i, 0))
    return pl.pallas_call(
        _inproj_kernel,
        grid=(n // tm,),
        in_specs=[row(d), _const_spec(gain.shape), _const_spec(wq.shape), _const_spec(wc.shape),
                  _const_spec(wkv.shape), _const_spec(wg.shape), _const_spec(wx.shape)],
        out_specs=[row(NSA_WIDTH), row(KV_WIDTH), row(KV_WIDTH), row(4 * KV_WIDTH), row(128),
                   row(wx.shape[1])],
        out_shape=[jax.ShapeDtypeStruct((n, NSA_WIDTH), BF16),
                   jax.ShapeDtypeStruct((n, KV_WIDTH), F32),
                   jax.ShapeDtypeStruct((n, KV_WIDTH), F32),
                   jax.ShapeDtypeStruct((n, 4 * KV_WIDTH), BF16),
                   jax.ShapeDtypeStruct((n, 128), F32),
                   jax.ShapeDtypeStruct((n, wx.shape[1]), F32)],
        compiler_params=_params(("parallel",)),
        name="inproj",
    )(x2, gain, wq, wc, wkv, wg, wx)


def _compress_kernel(kr_ref, vr_ref, pek_ref, pev_ref, wk_ref, wv_ref, kc_ref, vc_ref):
    kc_ref[...] = _dot((kr_ref[...] + pek_ref[...]).astype(BF16), wk_ref[...]).astype(BF16)
    vc_ref[...] = _dot((vr_ref[...] + pev_ref[...]).astype(BF16), wv_ref[...]).astype(BF16)


def _compress(kr, vr, pek, pev, wk, wv):
    nblk, width = kr.shape
    tm = 128
    row = lambda w: pl.BlockSpec((tm, w), lambda i: (i, 0))
    return pl.pallas_call(
        _compress_kernel,
        grid=(nblk // tm,),
        in_specs=[row(width), row(width), _const_spec(pek.shape), _const_spec(pev.shape),
                  _const_spec(wk.shape), _const_spec(wv.shape)],
        out_specs=[row(KV_WIDTH), row(KV_WIDTH)],
        out_shape=[jax.ShapeDtypeStruct((nblk, KV_WIDTH), BF16)] * 2,
        compiler_params=_params(("parallel",)),
        name="compress",
    )(kr, vr, pek, pev, wk, wv)


def _bias_tables_kernel(rb_ref, bma_ref, bmb_ref, bm1_ref, bm2_ref,
                        ta_ref, tb_ref, d1_ref, d2_ref):
    last = N_BUCKETS - 1
    for hd in range(NSA_HEADS):
        g, hl = divmod(hd, NSA_HPG)
        cols = slice(hl * TQ, (hl + 1) * TQ)
        far = rb_ref[last, hd]
        for bm_ref, out_ref, masked in ((bma_ref, ta_ref, False), (bmb_ref, tb_ref, True),
                                        (bm1_ref, d1_ref, False), (bm2_ref, d2_ref, False)):
            bm = bm_ref[...]
            acc = jnp.zeros(bm.shape, F32)
            for k in range(N_BUCKETS - 1):
                acc = jnp.where(bm == k, (rb_ref[k, hd] - far) * LOG2E, acc)
            if masked:
                acc = jnp.where(bm < 0, MASKV, acc)
            out_ref[g, :, cols] = acc


def _bias_tables(rel_bias):
    ki = np.arange(CK)[:, None]
    qi = np.arange(TQ)[None, :]
    bma = _bucket_of_distance(qi - ki + CK)
    bmb = np.where(qi >= ki, _bucket_of_distance(qi - ki), -1).astype(np.int32)
    r = (np.arange(TQ) + 1) % CMP_BLOCK
    bm1 = np.broadcast_to(_bucket_of_distance(r)[None, :], (8, TQ)).astype(np.int32)
    bm2 = np.broadcast_to(_bucket_of_distance(r + CMP_BLOCK)[None, :], (8, TQ)).astype(np.int32)
    return pl.pallas_call(
        _bias_tables_kernel,
        in_specs=[pl.BlockSpec(memory_space=pltpu.SMEM)] + [pl.BlockSpec(memory_space=pltpu.VMEM)] * 4,
        out_specs=[pl.BlockSpec(memory_space=pltpu.VMEM)] * 4,
        out_shape=[jax.ShapeDtypeStruct((NSA_KV_HEADS, CK, ROWS), F32),
                   jax.ShapeDtypeStruct((NSA_KV_HEADS, CK, ROWS), F32),
                   jax.ShapeDtypeStruct((NSA_KV_HEADS, 8, ROWS), F32),
                   jax.ShapeDtypeStruct((NSA_KV_HEADS, 8, ROWS), F32)],
        name="bias_tables",
    )(rel_bias, jnp.asarray(bma), jnp.asarray(bmb), jnp.asarray(bm1), jnp.asarray(bm2))


def _attend_pass(chains, exact_items, depth):
    items = [(ci, chunk) for k in range(max(len(c[0]) for c in chains))
             for ci, c in enumerate(chains) for chunk in c[0][k:k + 1]]

    def score(item):
        keys, q, bias, _ = item[1]
        s = _dot(keys, q)
        return s if bias is None else s + bias

    scores = [score(item) for item in items[:depth]]
    start = [(m_ref[...], acc_ref[...]) for _, m_ref, acc_ref in chains]
    state = list(start)
    tops = [None] * len(chains)
    seen = [0] * len(chains)
    for i, (ci, chunk) in enumerate(items):
        s = scores[i]
        m, acc = state[ci]
        top = jnp.max(s, axis=0, keepdims=True)
        if exact_items is None or seen[ci] < exact_items:
            m_new = jnp.maximum(m, top)
            p = jnp.exp2(s - m_new).astype(BF16)
            state[ci] = (m_new, jnp.exp2(m - m_new) * acc + _dot(chunk[3], p))
        else:
            tops[ci] = top if tops[ci] is None else jnp.maximum(tops[ci], top)
            state[ci] = (m, acc + _dot(chunk[3], jnp.exp2(s - m).astype(BF16)))
        seen[ci] += 1
        if i + depth < len(items):
            scores.append(score(items[i + depth]))
    rise = [top - m for top, (m, _) in zip(tops, state) if top is not None]
    moved = []
    for (m, acc), top in zip(state, tops):
        if top is not None:
            m_new = jnp.maximum(m, top)
            m, acc = m_new, jnp.exp2(m - m_new) * acc
        moved.append((m, acc))
    return moved, rise


def _commit(chains, state):
    for (_, m_ref, acc_ref), (m, acc) in zip(chains, state):
        m_ref[...] = m
        acc_ref[...] = acc


def _attend(chains, depth):
    _commit(chains, _attend_pass(chains, None, depth)[0])


def _attend_from_reference(chains, depth, exact_items=0):
    state, rise = _attend_pass(chains, exact_items, depth)
    worst = rise[0]
    for r in rise[1:]:
        worst = jnp.maximum(worst, r)
    safe = jnp.max(worst) <= MAX_RISE

    @pl.when(safe)
    def _():
        _commit(chains, state)

    @pl.when(jnp.logical_not(safe))
    def _():
        _attend(chains, depth)


def _attn_kernel(q_ref, gt_ref, kc_ref, vc_ref, ks_ref, vs_ref, kw_ref, vw_ref,
                 ta_ref, tb_ref, d1_ref, d2_ref, out_ref,
                 ka_ref, vst_ref, vwt_ref, vct_ref, qa_ref, oc_ref, ms_ref, accs_ref, mw_ref, accw_ref, outt_ref):
    qt = pl.program_id(1)
    n_blocks = kc_ref.shape[1]
    blocks_per_tile = TQ // CMP_BLOCK

    @pl.when(qt == 0)
    def _():
        ones_rows = jnp.where(lax.broadcasted_iota(jnp.int32, (VT_ROWS - HEAD_DIM, CK), 0) == 0,
                              1.0, 0.0).astype(BF16)

        def build(c, carry):
            rs = pl.ds(pl.multiple_of(c * CK, CK), CK)
            blk = lax.broadcasted_iota(jnp.int32, (CK, 128), 1)
            key_blk = c * (CK // CMP_BLOCK) + (lax.broadcasted_iota(jnp.int32, (CK, 128), 0) >> 6)
            ka_ref[rs, 0:128] = jnp.where(blk == key_blk, 1.0, 0.0).astype(BF16)
            ka_ref[rs, 128:256] = ks_ref[0, rs, :]
            vs_t = vs_ref[0, rs, :].astype(F32).T
            vw_t = vw_ref[0, rs, :].astype(F32).T
            for g in range(NSA_KV_HEADS):
                gs = slice(g * HEAD_DIM, (g + 1) * HEAD_DIM)
                vst_ref[c, g, 0:HEAD_DIM, :] = vs_t[gs].astype(BF16)
                vst_ref[c, g, HEAD_DIM:, :] = ones_rows
                vwt_ref[c, g, 0:HEAD_DIM, :] = vw_t[gs].astype(BF16)
                vwt_ref[c, g, HEAD_DIM:, :] = ones_rows
            return carry
        lax.fori_loop(0, ks_ref.shape[1] // CK, build, 0)
        vct_ref[...] = vc_ref[0].astype(F32).T.astype(BF16)

    q_t = q_ref[0].astype(F32).T
    gates_t = jax.nn.sigmoid(gt_ref[0]).T
    kc = kc_ref[0]
    own_rows = pl.ds(pl.multiple_of(qt * CK, CK), CK)
    own_ks_t = ka_ref[own_rows, 128:256].astype(F32).T
    own_kw_t = kw_ref[0, own_rows, :].astype(F32).T

    q_loc = lax.broadcasted_iota(jnp.int32, (1, ROWS), 1) & (TQ - 1)
    n1 = qt * blocks_per_tile + ((q_loc + 1) >> 6) - 1
    cur = qt * blocks_per_tile + (lax.broadcasted_iota(jnp.int32, (1, TQ), 1) >> 6)
    zeros_half = jnp.zeros((HEAD_DIM, ROWS), BF16)
    groups = range(NSA_KV_HEADS)

    def chunk_rows(c):
        return pl.ds(pl.multiple_of(c * CK, CK), CK)

    for g in groups:
        gs = slice(g * HEAD_DIM, (g + 1) * HEAD_DIM)
        q_g = jnp.concatenate(
            [q_t[(g * NSA_HPG + hl) * HEAD_DIM:(g * NSA_HPG + hl + 1) * HEAD_DIM, :]
             for hl in range(NSA_HPG)], axis=1).astype(BF16)
        qa_ref[g, 128:256, :] = jnp.concatenate([q_g, zeros_half] if g == 0 else [zeros_half, q_g], axis=0)

        q_f = q_g.astype(F32)
        for m_ref, acc_ref, own_keys in ((ms_ref, accs_ref, own_ks_t), (mw_ref, accw_ref, own_kw_t)):
            own = jnp.concatenate([own_keys[gs]] * NSA_HPG, axis=1)
            m_ref[g] = jnp.sum(q_f * own, axis=0, keepdims=True)
            acc_ref[g] = jnp.zeros((VT_ROWS, ROWS), F32)

    def compress_and_select(nb):
        n_iota = lax.broadcasted_iota(jnp.int32, (nb, ROWS), 0)
        j_s = lax.broadcasted_iota(jnp.int32, (nb, TQ), 0)
        j_f = j_s.astype(F32)
        for g in groups:
            gs = slice(g * HEAD_DIM, (g + 1) * HEAD_DIM)
            s_c = _dot(kc[:nb], qa_ref[g, 128:256, :])
            bias_c = (jnp.where(n_iota == n1, d1_ref[g, 0:1, :], 0.0)
                      + jnp.where(n_iota == n1 - 1, d2_ref[g, 0:1, :], 0.0))
            valid_c = n_iota <= n1
            l_c = jnp.where(valid_c, s_c + bias_c, NEG)
            m_c = jnp.max(l_c, axis=0, keepdims=True)
            e_c = jnp.where(valid_c, jnp.exp2(l_c - m_c), 0.0)
            p_c = e_c / jnp.maximum(jnp.sum(e_c, axis=0, keepdims=True), 1e-30)
            p_all = p_c.astype(BF16)
            if nb < n_blocks:
                p_all = jnp.concatenate([p_all, jnp.zeros((n_blocks - nb, ROWS), BF16)], axis=0)
            oc_ref[g] = _dot(vct_ref[gs, :], p_all)

            imp = p_c[:, 0:TQ]
            for hl in range(1, NSA_HPG):
                imp = imp + p_c[:, hl * TQ:(hl + 1) * TQ]
            forced = (j_s == 0) | (j_s == cur) | (j_s == cur - 1)
            candidates = jnp.where(forced, NEG, jnp.where(j_s <= cur, imp, NEG))
            work = candidates
            for _ in range(SEL_TOPN - 3):
                mx = jnp.max(work, axis=0, keepdims=True)
                first = jnp.min(jnp.where(work == mx, j_f, float(nb)), axis=0, keepdims=True)
                work = jnp.where(j_f == first, -jnp.inf, work)
            picked_mask = jnp.where(candidates > NEG / 2, jnp.where(work == -jnp.inf, 0.0, MASKV), MASKV)
            sel_mask = jnp.where(forced, 0.0, picked_mask).astype(BF16)
            qa_ref[g, 0:nb, :] = jnp.concatenate([sel_mask] * NSA_HPG, axis=1)
            if nb < n_blocks:
                qa_ref[g, nb:n_blocks, :] = jnp.full((n_blocks - nb, ROWS), MASKV, BF16)

    tiles_per_span = n_blocks // SELECT_SPANS // blocks_per_tile
    for k in range(SELECT_SPANS):
        @pl.when(qt // tiles_per_span == k)
        def _(k=k):
            compress_and_select((k + 1) * n_blocks // SELECT_SPANS)

    def sel_chains(cs, biases=None):
        keys = [ka_ref[chunk_rows(c), :] for c in cs]
        biases = biases or [None] * len(cs)
        return [([(k, qa_ref[g], bias if bias is None else bias[g], vst_ref[c, g])
                  for k, c, bias in zip(keys, cs, biases)], ms_ref.at[g], accs_ref.at[g]) for g in groups]

    def win_chains(cs, biases):
        keys = [kw_ref[0, chunk_rows(c), :] for c in cs]
        return [([(k, qa_ref[g, 128:256, :], bias[g], vwt_ref[c, g])
                  for k, c, bias in zip(keys, cs, biases)], mw_ref.at[g], accw_ref.at[g]) for g in groups]

    @pl.when(qt >= 2)
    def _():
        w2_mask = jnp.where(q_loc < lax.broadcasted_iota(jnp.int32, (CK, ROWS), 0), 0.0, MASKV)
        ta, tb = [ta_ref[g] for g in groups], [tb_ref[g] for g in groups]
        _attend_from_reference(sel_chains([qt, qt - 1], [tb, ta])
                               + win_chains([qt, qt - 1, qt - 2], [tb, ta, [w2_mask] * NSA_KV_HEADS]), TAIL_DEPTH)

    @pl.when(qt == 1)
    def _():
        ta, tb = [ta_ref[g] for g in groups], [tb_ref[g] for g in groups]
        _attend_from_reference(sel_chains([1, 0], [tb, ta]) + win_chains([1, 0], [tb, ta]), TAIL_DEPTH)

    @pl.when(qt == 0)
    def _():
        tb = [tb_ref[g] for g in groups]
        _attend_from_reference(sel_chains([0], [tb]) + win_chains([0], [tb]), TAIL_DEPTH)

    n_far = jnp.maximum(qt - 1, 0)

    def far_block(i, carry):
        _attend_from_reference(sel_chains([FAR_BLOCK * i + k for k in range(FAR_BLOCK)]), FAR_DEPTH)
        return carry
    lax.fori_loop(0, n_far // FAR_BLOCK, far_block, 0)
    done = (n_far // FAR_BLOCK) * FAR_BLOCK
    size = FAR_BLOCK // 2
    while size:
        @pl.when((n_far & size) != 0)
        def _(done=done, size=size):
            _attend_from_reference(sel_chains([done + k for k in range(size)]), FAR_DEPTH)
        done = done + (n_far & size)
        size //= 2

    for g in range(NSA_KV_HEADS):
        o_s = accs_ref[g, 0:HEAD_DIM, :] / accs_ref[g, HEAD_DIM:HEAD_DIM + 1, :]
        o_w = accw_ref[g, 0:HEAD_DIM, :] / accw_ref[g, HEAD_DIM:HEAD_DIM + 1, :]

        def gate(branch):
            return jnp.concatenate(
                [gates_t[(g * NSA_HPG + hl) * 3 + branch:(g * NSA_HPG + hl) * 3 + branch + 1, :]
                 for hl in range(NSA_HPG)], axis=1)
        mixed = gate(0) * oc_ref[g] + gate(1) * o_s + gate(2) * o_w
        for hl in range(NSA_HPG):
            hd = g * NSA_HPG + hl
            outt_ref[hd * HEAD_DIM:(hd + 1) * HEAD_DIM, :] = mixed[:, hl * TQ:(hl + 1) * TQ]

    out_ref[0] = outt_ref[...].T


def _attention(q, gates, kc, vc, kv, tabs):
    b, s, _ = q.shape
    n_blocks = s // CMP_BLOCK
    nch = s // CK
    ta, tb, d1, d2 = tabs
    tile = lambda w: pl.BlockSpec((1, TQ, w), lambda bi, qi: (bi, qi, 0))
    per_b = lambda r, w, col: pl.BlockSpec((1, r, w), lambda bi, qi: (bi, 0, col), pipeline_mode=pl.Buffered(1))
    return pl.pallas_call(
        _attn_kernel,
        grid=(b, s // TQ),
        in_specs=[tile(NSA_WIDTH), tile(128),
                  per_b(n_blocks, KV_WIDTH, 0), per_b(n_blocks, KV_WIDTH, 0),
                  per_b(s, KV_WIDTH, 0), per_b(s, KV_WIDTH, 1), per_b(s, KV_WIDTH, 2), per_b(s, KV_WIDTH, 3),
                  _const_spec(ta.shape), _const_spec(tb.shape), _const_spec(d1.shape), _const_spec(d2.shape)],
        out_specs=tile(NSA_WIDTH),
        out_shape=jax.ShapeDtypeStruct((b, s, NSA_WIDTH), F32),
        scratch_shapes=[pltpu.VMEM((s, CK), BF16),
                        pltpu.VMEM((nch, NSA_KV_HEADS, VT_ROWS, CK), BF16),
                        pltpu.VMEM((nch, NSA_KV_HEADS, VT_ROWS, CK), BF16),
                        pltpu.VMEM((KV_WIDTH, n_blocks), BF16),
                        pltpu.VMEM((NSA_KV_HEADS, CK, ROWS), BF16),
                        pltpu.VMEM((NSA_KV_HEADS, HEAD_DIM, ROWS), F32),
                        pltpu.VMEM((NSA_KV_HEADS, 1, ROWS), F32), pltpu.VMEM((NSA_KV_HEADS, VT_ROWS, ROWS), F32),
                        pltpu.VMEM((NSA_KV_HEADS, 1, ROWS), F32), pltpu.VMEM((NSA_KV_HEADS, VT_ROWS, ROWS), F32),
                        pltpu.VMEM((NSA_WIDTH, TQ), F32)],
        compiler_params=_params(("parallel", "arbitrary")),
        name="nsa_attention",
    )(q, gates, kc, vc, kv, kv, kv, kv, ta, tb, d1, d2)


def _lru_kernel(xgr_ref, cw_ref, cb_ref, wa_ref, ba_ref, wx_ref, bx_ref, lam_ref,
                out_ref, xbuf_ref, h_ref):
    t = pl.program_id(1)
    ts = LRU_TILE
    w = out_ref.shape[2]

    @pl.when(t == 0)
    def _():
        xbuf_ref[0:8, :] = jnp.zeros((8, w), F32)
        h_ref[...] = jnp.zeros_like(h_ref)

    xg = xgr_ref[0, :, :w]
    xbuf_ref[8:, :] = xgr_ref[0, :, w:]
    xc = cb_ref[...] + xbuf_ref[8:, :] * cw_ref[CONV_WIDTH - 1:CONV_WIDTH, :]
    for k in range(1, CONV_WIDTH):
        xc = xc + xbuf_ref[8 - k:8 - k + ts, :] * cw_ref[CONV_WIDTH - 1 - k:CONV_WIDTH - k, :]
    xbuf_ref[0:8, :] = xbuf_ref[ts:ts + 8, :]

    xcb = xc.astype(BF16)
    r = jax.nn.sigmoid(_dot(xcb, wa_ref[...]) + ba_ref[...])
    i = jax.nn.sigmoid(_dot(xcb, wx_ref[...]) + bx_ref[...])
    nl = -lam_ref[...]
    softplus = jnp.maximum(nl, 0.0) + jnp.log(1.0 + jnp.exp(-jnp.abs(nl)))
    log_a = (-LRU_C * softplus) * r
    a = jnp.exp(log_a)
    bv = jnp.sqrt(1.0 - a * a) * (i * xc)

    row8 = lax.broadcasted_iota(jnp.int32, (8, w), 0)
    y = jax.nn.gelu(xg)
    hc = h_ref[0:1, :]
    for grp in range(ts // 8):
        rs = slice(8 * grp, 8 * grp + 8)
        ag, bg = a[rs], bv[rs]
        sh = 1
        while sh < 8:
            a_prev = jnp.where(row8 >= sh, pltpu.roll(ag, sh, 0), 1.0)
            b_prev = jnp.where(row8 >= sh, pltpu.roll(bg, sh, 0), 0.0)
            bg = ag * b_prev + bg
            ag = ag * a_prev
            sh *= 2
        hg = bg + ag * hc
        out_ref[0, rs, :] = hg * y[rs]
        hc = hg[7:8]
    h_ref[0:1, :] = hc


def _lru(xgr, cw, cb, wa, ba, wx, bx, lam):
    b, s, w2 = xgr.shape
    w = w2 // 2
    ts = LRU_TILE
    return pl.pallas_call(
        _lru_kernel,
        grid=(b, s // ts),
        in_specs=[pl.BlockSpec((1, ts, w2), lambda bi, ti: (bi, ti, 0)),
                  _const_spec(cw.shape), _const_spec(cb.shape), _const_spec(wa.shape), _const_spec(ba.shape),
                  _const_spec(wx.shape), _const_spec(bx.shape), _const_spec(lam.shape)],
        out_specs=pl.BlockSpec((1, ts, w), lambda bi, ti: (bi, ti, 0)),
        out_shape=jax.ShapeDtypeStruct((b, s, w), F32),
        scratch_shapes=[pltpu.VMEM((ts + 8, w), F32), pltpu.VMEM((8, w), F32)],
        compiler_params=_params(("parallel", "arbitrary")),
        name="rg_lru",
    )(xgr, cw, cb, wa, ba, wx, bx, lam)


def _post_kernel(x_ref, att_ref, lru_ref, kt_ref, v_ref,
                 ga_ref, gl_ref, wa_ref, wl_ref, gmix_ref,
                 gxpre_ref, wq_ref, wo_ref, gxpost_ref,
                 gmpre_ref, w1_ref, w2_ref, gmpost_ref, out_ref):
    tm, d = x_ref.shape[1], x_ref.shape[2]
    parts = [slice(k * tm // ROW_SPLIT, (k + 1) * tm // ROW_SPLIT) for k in range(ROW_SPLIT)]
    xs = [x_ref[0, r, :] for r in parts]
    a = [_rms(att_ref[0, r, :], ga_ref[...]).astype(BF16) for r in parts]
    l = [_rms(lru_ref[0, r, :], gl_ref[...]).astype(BF16) for r in parts]
    mixed = [_dot(ak, wa_ref[...]) + _dot(lk, wl_ref[...]) for ak, lk in zip(a, l)]
    xs = [x + _rms(mk, gmix_ref[...]) for x, mk in zip(xs, mixed)]

    dh = d // X_HEADS
    hq = [_rms(x, gxpre_ref[...]).astype(BF16) for x in xs]
    cq = [(_dot(h, wq_ref[...]) * (dh ** -0.5)).astype(BF16) for h in hq]
    outs = [[] for _ in parts]
    for hh in range(X_HEADS):
        hs = slice(hh * dh, (hh + 1) * dh)
        scores = [_dot(c[:, hs], kt_ref[0, hs, :]) for c in cq]
        for out, s in zip(outs, scores):
            e = jnp.exp(s - jnp.max(s, axis=1, keepdims=True))
            p = e / jnp.sum(e, axis=1, keepdims=True)
            out.append(_dot(p.astype(BF16), v_ref[0, :, hs]).astype(BF16))
    co = [_dot(jnp.concatenate(out, axis=1), wo_ref[...]) for out in outs]
    xs = [x + _rms(c, gxpost_ref[...]) for x, c in zip(xs, co)]

    hm = [_rms(x, gmpre_ref[...]).astype(BF16) for x in xs]
    accs = [jnp.zeros(x.shape, F32) for x in xs]
    for c in range(w1_ref.shape[1] // FF_CHUNK):
        cs = slice(c * FF_CHUNK, (c + 1) * FF_CHUNK)
        us = [jnp.maximum(_dot(h, w1_ref[:, cs]), 0.0) for h in hm]
        accs = [acc + _dot((u * u).astype(BF16), w2_ref[cs, :]) for acc, u in zip(accs, us)]
    for r, x, acc in zip(parts, xs, accs):
        out_ref[0, r, :] = x + _rms(acc, gmpost_ref[...])


def _post(x, att, lru, kt, v, gains_and_weights):
    b, s, d = x.shape
    m = v.shape[1]
    tm = ROW_TILE
    row = lambda w: pl.BlockSpec((1, tm, w), lambda bi, i: (bi, i, 0))
    resident = lambda a: pl.BlockSpec(a.shape, lambda *_: (0,) * a.ndim, pipeline_mode=pl.Buffered(1))
    return pl.pallas_call(
        _post_kernel,
        grid=(b, s // tm),
        in_specs=[row(d), row(att.shape[2]), row(lru.shape[2]),
                  pl.BlockSpec((1, d, m), lambda bi, i: (bi, 0, 0)),
                  pl.BlockSpec((1, m, d), lambda bi, i: (bi, 0, 0))]
                 + [resident(a) for a in gains_and_weights],
        out_specs=row(d),
        out_shape=jax.ShapeDtypeStruct((b, s, d), F32),
        compiler_params=_params(("parallel", "parallel")),
        name="post_mixer",
    )(x, att, lru, kt, v, *gains_and_weights)


def _memkv_kernel(mem_ref, g_ref, wk_ref, wv_ref, kt_ref, v_ref):
    mn = _rms(mem_ref[0], g_ref[...]).astype(BF16)
    kt_ref[0] = _dot(mn, wk_ref[...]).T.astype(BF16)
    v_ref[0] = _dot(mn, wv_ref[...]).astype(BF16)


def _memkv(mem, g, wk, wv):
    b, m, d = mem.shape
    return pl.pallas_call(
        _memkv_kernel,
        grid=(b,),
        in_specs=[pl.BlockSpec((1, m, d), lambda i: (i, 0, 0)), _const_spec(g.shape),
                  _const_spec(wk.shape), _const_spec(wv.shape)],
        out_specs=[pl.BlockSpec((1, d, m), lambda i: (i, 0, 0)), pl.BlockSpec((1, m, d), lambda i: (i, 0, 0))],
        out_shape=[jax.ShapeDtypeStruct((b, d, m), BF16), jax.ShapeDtypeStruct((b, m, d), BF16)],
        compiler_params=_params(("parallel",)),
        name="mem_kv",
    )(mem, g, wk, wv)


def _block_structured(w):
    eye = jnp.eye(NSA_KV_HEADS, dtype=w.dtype)
    big = w[:, None, :, None, :] * eye[None, :, None, :, None]
    return big.reshape(CMP_BLOCK * KV_WIDTH, KV_WIDTH).astype(BF16)


def _block_diag(w):
    n, d, e = w.shape
    eye = jnp.eye(n, dtype=w.dtype)
    return (w[:, :, None, :] * eye[:, None, :, None]).reshape(n * d, n * e).astype(BF16)


def kernel(x, mem, rel_bias, ln_mix_pre, ln_mix_post, w_in, cmp_pe_k, cmp_pe_v, cmp_w_k, cmp_w_v,
           conv_w, conv_b, lru_wa, lru_ba, lru_wx, lru_bx, lru_lambda, gn_attn, gn_lru, w_out,
           ln_x_pre, ln_x_post, ln_mem, xq, xkv, xo, ln_mlp_pre, ln_mlp_post, mlp_w1, mlp_w2):
    b, s, d = x.shape
    depth = w_in.shape[0]
    n = b * s
    lru_w = conv_w.shape[2]
    assert s % TQ == 0 and s % LRU_TILE == 0 and n % (2 * ROW_TILE) == 0 and TQ == CK
    assert s // CMP_BLOCK == 128, "selection mask layout assumes 128 compression blocks"
    row1 = lambda v: v.reshape(1, -1)

    tabs = _bias_tables(rel_bias)
    x2 = x.reshape(n, d)
    o_q = NSA_WIDTH
    o_g = o_q + 6 * KV_WIDTH
    o_x = o_g + 3 * NSA_HEADS
    for l in range(depth):
        wl = w_in[l]
        wq = wl[:, :o_q].astype(BF16)
        wc = wl[:, o_q:o_q + 2 * KV_WIDTH].astype(BF16)
        wkv = wl[:, o_q + 2 * KV_WIDTH:o_g].astype(BF16)
        wg = jnp.pad(wl[:, o_g:o_x], ((0, 0), (0, 128 - 3 * NSA_HEADS))).astype(BF16)
        wx = wl[:, o_x:].astype(BF16)
        q, kcr, vcr, kv, gt, xgr = _inproj(x2, row1(ln_mix_pre[l]), wq, wc, wkv, wg, wx)

        nblk = n // CMP_BLOCK
        pek = jnp.tile(cmp_pe_k[l], (1, NSA_KV_HEADS)).reshape(1, -1)
        pev = jnp.tile(cmp_pe_v[l], (1, NSA_KV_HEADS)).reshape(1, -1)
        kc, vc = _compress(kcr.reshape(nblk, CMP_BLOCK * KV_WIDTH), vcr.reshape(nblk, CMP_BLOCK * KV_WIDTH),
                           pek, pev, _block_structured(cmp_w_k[l]), _block_structured(cmp_w_v[l]))

        att = _attention(q.reshape(b, s, NSA_WIDTH), gt.reshape(b, s, 128),
                         kc.reshape(b, s // CMP_BLOCK, KV_WIDTH), vc.reshape(b, s // CMP_BLOCK, KV_WIDTH),
                         kv.reshape(b, s, 4 * KV_WIDTH), tabs)
        lru = _lru(xgr.reshape(b, s, 2 * lru_w), conv_w[l], row1(conv_b[l]),
                   _block_diag(lru_wa[l]), row1(lru_ba[l]), _block_diag(lru_wx[l]), row1(lru_bx[l]),
                   row1(lru_lambda[l]))

        wo = w_out[l].astype(BF16)
        wkv_x = xkv[l].astype(BF16)
        kt, v = _memkv(mem, row1(ln_mem[l]), wkv_x[:, :d], wkv_x[:, d:])
        x2 = _post(x2.reshape(b, s, d), att, lru, kt, v,
                   (row1(gn_attn[l]), row1(gn_lru[l]), wo[:NSA_WIDTH], wo[NSA_WIDTH:], row1(ln_mix_post[l]),
                    row1(ln_x_pre[l]), xq[l].astype(BF16), xo[l].astype(BF16), row1(ln_x_post[l]),
                    row1(ln_mlp_pre[l]), mlp_w1[l].astype(BF16), mlp_w2[l].astype(BF16),
                    row1(ln_mlp_post[l]))).reshape(n, d)
    return x2.reshape(b, s, d)
```

```python
import functools
import math

import numpy as np
import jax
import jax.numpy as jnp
from jax import lax
from jax.experimental import pallas as pl
from jax.experimental.pallas import tpu as pltpu

HEAD_DIM = 64
NSA_HEADS = 8
NSA_KV_HEADS = 2
NSA_HPG = NSA_HEADS // NSA_KV_HEADS
NSA_WIDTH = NSA_HEADS * HEAD_DIM
KV_WIDTH = NSA_KV_HEADS * HEAD_DIM
CMP_BLOCK = 64
SEL_TOPN = 16
WINDOW = 512
FORCE_SCORE = 1e4
LRU_BLOCKS = 8
CONV_WIDTH = 4
LRU_C = 8.0
X_HEADS = 4
N_BUCKETS = 32
MAX_DISTANCE = 128
EPS = 1e-6
NEG = -1e30

TQ = 256
CK = 256
ROWS = NSA_HPG * TQ
VT_ROWS = HEAD_DIM + 16
LOG2E = 1.4426950408889634
FAR_BLOCK = 8
FAR_DEPTH = 4
TAIL_DEPTH = 3
MASKV = -(2.0 ** 100)
MAX_RISE = 64.0
ROW_TILE = 512
FF_CHUNK = 1024
SELECT_SPANS = 8
ROW_SPLIT = 2
LRU_TILE = 1024
VMEM_LIMIT = 56 * 1024 * 1024

F32 = jnp.float32
BF16 = jnp.bfloat16


def _bucket_of_distance(d):
    max_exact = N_BUCKETS // 2
    d = np.maximum(d, 0)
    df = np.maximum(d, 1).astype(np.float64)
    large = max_exact + (np.log(df / max_exact) / math.log(MAX_DISTANCE / max_exact)
                         * (N_BUCKETS - max_exact)).astype(np.int32)
    large = np.minimum(large, N_BUCKETS - 1)
    return np.where(d < max_exact, d, large).astype(np.int32)


def _rms(x, g):
    return x * lax.rsqrt(jnp.mean(x * x, axis=-1, keepdims=True) + EPS) * g


def _dot(a, b):
    return jnp.dot(a, b, preferred_element_type=F32)


def _params(sem):
    return pltpu.CompilerParams(dimension_semantics=sem, vmem_limit_bytes=VMEM_LIMIT)


def _const_spec(shape):
    nd = len(shape)
    return pl.BlockSpec(shape, lambda *_: (0,) * nd)


def _inproj_kernel(x_ref, g_ref, wq_ref, wc_ref, wkv_ref, wg_ref, wx_ref,
                   q_ref, kcr_ref, vcr_ref, kv_ref, gt_ref, xgr_ref):
    h = _rms(x_ref[...], g_ref[...]).astype(BF16)
    q_ref[...] = (_dot(h, wq_ref[...]) * (HEAD_DIM ** -0.5 * LOG2E)).astype(BF16)
    c = _dot(h, wc_ref[...])
    kcr_ref[...] = c[:, :KV_WIDTH]
    vcr_ref[...] = c[:, KV_WIDTH:]
    kv_ref[...] = _dot(h, wkv_ref[...]).astype(BF16)
    gt_ref[...] = _dot(h, wg_ref[...])
    xgr_ref[...] = _dot(h, wx_ref[...])


def _inproj(x2, gain, wq, wc, wkv, wg, wx):
    n, d = x2.shape
    tm = 2 * ROW_TILE
    row = lambda w: pl.BlockSpec((tm, w), lambda i: (i, 0))
    return pl.pallas_call(
        _inproj_kernel,
        grid=(n // tm,),
        in_specs=[row(d), _const_spec(gain.shape), _const_spec(wq.shape), _const_spec(wc.shape),
                  _const_spec(wkv.shape), _const_spec(wg.shape), _const_spec(wx.shape)],
        out_specs=[row(NSA_WIDTH), row(KV_WIDTH), row(KV_WIDTH), row(4 * KV_WIDTH), row(128),
                   row(wx.shape[1])],
        out_shape=[jax.ShapeDtypeStruct((n, NSA_WIDTH), BF16),
                   jax.ShapeDtypeStruct((n, KV_WIDTH), F32),
                   jax.ShapeDtypeStruct((n, KV_WIDTH), F32),
                   jax.ShapeDtypeStruct((n, 4 * KV_WIDTH), BF16),
                   jax.ShapeDtypeStruct((n, 128), F32),
                   jax.ShapeDtypeStruct((n, wx.shape[1]), F32)],
        compiler_params=_params(("parallel",)),
        name="inproj",
    )(x2, gain, wq, wc, wkv, wg, wx)


def _compress_kernel(kr_ref, vr_ref, pek_ref, pev_ref, wk_ref, wv_ref, kc_ref, vc_ref):
    kc_ref[...] = _dot((kr_ref[...] + pek_ref[...]).astype(BF16), wk_ref[...]).astype(BF16)
    vc_ref[...] = _dot((vr_ref[...] + pev_ref[...]).astype(BF16), wv_ref[...]).astype(BF16)


def _compress(kr, vr, pek, pev, wk, wv):
    nblk, width = kr.shape
    tm = 128
    row = lambda w: pl.BlockSpec((tm, w), lambda i: (i, 0))
    return pl.pallas_call(
        _compress_kernel,
        grid=(nblk // tm,),
        in_specs=[row(width), row(width), _const_spec(pek.shape), _const_spec(pev.shape),
                  _const_spec(wk.shape), _const_spec(wv.shape)],
        out_specs=[row(KV_WIDTH), row(KV_WIDTH)],
        out_shape=[jax.ShapeDtypeStruct((nblk, KV_WIDTH), BF16)] * 2,
        compiler_params=_params(("parallel",)),
        name="compress",
    )(kr, vr, pek, pev, wk, wv)


def _bias_tables_kernel(rb_ref, bma_ref, bmb_ref, bm1_ref, bm2_ref,
                        ta_ref, tb_ref, d1_ref, d2_ref):
    last = N_BUCKETS - 1
    for hd in range(NSA_HEADS):
        g, hl = divmod(hd, NSA_HPG)
        cols = slice(hl * TQ, (hl + 1) * TQ)
        far = rb_ref[last, hd]
        for bm_ref, out_ref, masked in ((bma_ref, ta_ref, False), (bmb_ref, tb_ref, True),
                                        (bm1_ref, d1_ref, False), (bm2_ref, d2_ref, False)):
            bm = bm_ref[...]
            acc = jnp.zeros(bm.shape, F32)
            for k in range(N_BUCKETS - 1):
                acc = jnp.where(bm == k, (rb_ref[k, hd] - far) * LOG2E, acc)
            if masked:
                acc = jnp.where(bm < 0, MASKV, acc)
            out_ref[g, :, cols] = acc


def _bias_tables(rel_bias):
    ki = np.arange(CK)[:, None]
    qi = np.arange(TQ)[None, :]
    bma = _bucket_of_distance(qi - ki + CK)
    bmb = np.where(qi >= ki, _bucket_of_distance(qi - ki), -1).astype(np.int32)
    r = (np.arange(TQ) + 1) % CMP_BLOCK
    bm1 = np.broadcast_to(_bucket_of_distance(r)[None, :], (8, TQ)).astype(np.int32)
    bm2 = np.broadcast_to(_bucket_of_distance(r + CMP_BLOCK)[None, :], (8, TQ)).astype(np.int32)
    return pl.pallas_call(
        _bias_tables_kernel,
        in_specs=[pl.BlockSpec(memory_space=pltpu.SMEM)] + [pl.BlockSpec(memory_space=pltpu.VMEM)] * 4,
        out_specs=[pl.BlockSpec(memory_space=pltpu.VMEM)] * 4,
        out_shape=[jax.ShapeDtypeStruct((NSA_KV_HEADS, CK, ROWS), F32),
                   jax.ShapeDtypeStruct((NSA_KV_HEADS, CK, ROWS), F32),
                   jax.ShapeDtypeStruct((NSA_KV_HEADS, 8, ROWS), F32),
                   jax.ShapeDtypeStruct((NSA_KV_HEADS, 8, ROWS), F32)],
        name="bias_tables",
    )(rel_bias, jnp.asarray(bma), jnp.asarray(bmb), jnp.asarray(bm1), jnp.asarray(bm2))


def _attend_pass(chains, exact_items, depth):
    items = [(ci, chunk) for k in range(max(len(c[0]) for c in chains))
             for ci, c in enumerate(chains) for chunk in c[0][k:k + 1]]

    def score(item):
        keys, q, bias, _ = item[1]
        s = _dot(keys, q)
        return s if bias is None else s + bias

    scores = [score(item) for item in items[:depth]]
    start = [(m_ref[...], acc_ref[...]) for _, m_ref, acc_ref in chains]
    state = list(start)
    tops = [None] * len(chains)
    seen = [0] * len(chains)
    for i, (ci, chunk) in enumerate(items):
        s = scores[i]
        m, acc = state[ci]
        top = jnp.max(s, axis=0, keepdims=True)
        if exact_items is None or seen[ci] < exact_items:
            m_new = jnp.maximum(m, top)
            p = jnp.exp2(s - m_new).astype(BF16)
            state[ci] = (m_new, jnp.exp2(m - m_new) * acc + _dot(chunk[3], p))
        else:
            tops[ci] = top if tops[ci] is None else jnp.maximum(tops[ci], top)
            state[ci] = (m, acc + _dot(chunk[3], jnp.exp2(s - m).astype(BF16)))
        seen[ci] += 1
        if i + depth < len(items):
            scores.append(score(items[i + depth]))
    rise = [top - m for top, (m, _) in zip(tops, state) if top is not None]
    moved = []
    for (m, acc), top in zip(state, tops):
        if top is not None:
            m_new = jnp.maximum(m, top)
            m, acc = m_new, jnp.exp2(m - m_new) * acc
        moved.append((m, acc))
    return moved, rise


def _commit(chains, state):
    for (_, m_ref, acc_ref), (m, acc) in zip(chains, state):
        m_ref[...] = m
        acc_ref[...] = acc


def _attend(chains, depth):
    _commit(chains, _attend_pass(chains, None, depth)[0])


def _attend_from_reference(chains, depth, exact_items=0):
    state, rise = _attend_pass(chains, exact_items, depth)
    worst = rise[0]
    for r in rise[1:]:
        worst = jnp.maximum(worst, r)
    safe = jnp.max(worst) <= MAX_RISE

    @pl.when(safe)
    def _():
        _commit(chains, state)

    @pl.when(jnp.logical_not(safe))
    def _():
        _attend(chains, depth)


def _attn_kernel(q_ref, gt_ref, kc_ref, vc_ref, ks_ref, vs_ref, kw_ref, vw_ref,
                 ta_ref, tb_ref, d1_ref, d2_ref, out_ref,
                 ka_ref, vst_ref, vwt_ref, vct_ref, qa_ref, oc_ref, ms_ref, accs_ref, mw_ref, accw_ref, outt_ref):
    qt = pl.program_id(1)
    n_blocks = kc_ref.shape[1]
    blocks_per_tile = TQ // CMP_BLOCK

    @pl.when(qt == 0)
    def _():
        ones_rows = jnp.where(lax.broadcasted_iota(jnp.int32, (VT_ROWS - HEAD_DIM, CK), 0) == 0,
                              1.0, 0.0).astype(BF16)

        def build(c, carry):
            rs = pl.ds(pl.multiple_of(c * CK, CK), CK)
            blk = lax.broadcasted_iota(jnp.int32, (CK, 128), 1)
            key_blk = c * (CK // CMP_BLOCK) + (lax.broadcasted_iota(jnp.int32, (CK, 128), 0) >> 6)
            ka_ref[rs, 0:128] = jnp.where(blk == key_blk, 1.0, 0.0).astype(BF16)
            ka_ref[rs, 128:256] = ks_ref[0, rs, :]
            vs_t = vs_ref[0, rs, :].astype(F32).T
            vw_t = vw_ref[0, rs, :].astype(F32).T
            for g in range(NSA_KV_HEADS):
                gs = slice(g * HEAD_DIM, (g + 1) * HEAD_DIM)
                vst_ref[c, g, 0:HEAD_DIM, :] = vs_t[gs].astype(BF16)
                vst_ref[c, g, HEAD_DIM:, :] = ones_rows
                vwt_ref[c, g, 0:HEAD_DIM, :] = vw_t[gs].astype(BF16)
                vwt_ref[c, g, HEAD_DIM:, :] = ones_rows
            return carry
        lax.fori_loop(0, ks_ref.shape[1] // CK, build, 0)
        vct_ref[...] = vc_ref[0].astype(F32).T.astype(BF16)

    q_t = q_ref[0].astype(F32).T
    gates_t = jax.nn.sigmoid(gt_ref[0]).T
    kc = kc_ref[0]
    own_rows = pl.ds(pl.multiple_of(qt * CK, CK), CK)
    own_ks_t = ka_ref[own_rows, 128:256].astype(F32).T
    own_kw_t = kw_ref[0, own_rows, :].astype(F32).T

    q_loc = lax.broadcasted_iota(jnp.int32, (1, ROWS), 1) & (TQ - 1)
    n1 = qt * blocks_per_tile + ((q_loc + 1) >> 6) - 1
    cur = qt * blocks_per_tile + (lax.broadcasted_iota(jnp.int32, (1, TQ), 1) >> 6)
    zeros_half = jnp.zeros((HEAD_DIM, ROWS), BF16)
    groups = range(NSA_KV_HEADS)

    def chunk_rows(c):
        return pl.ds(pl.multiple_of(c * CK, CK), CK)

    for g in groups:
        gs = slice(g * HEAD_DIM, (g + 1) * HEAD_DIM)
        q_g = jnp.concatenate(
            [q_t[(g * NSA_HPG + hl) * HEAD_DIM:(g * NSA_HPG + hl + 1) * HEAD_DIM, :]
             for hl in range(NSA_HPG)], axis=1).astype(BF16)
        qa_ref[g, 128:256, :] = jnp.concatenate([q_g, zeros_half] if g == 0 else [zeros_half, q_g], axis=0)

        q_f = q_g.astype(F32)
        for m_ref, acc_ref, own_keys in ((ms_ref, accs_ref, own_ks_t), (mw_ref, accw_ref, own_kw_t)):
            own = jnp.concatenate([own_keys[gs]] * NSA_HPG, axis=1)
            m_ref[g] = jnp.sum(q_f * own, axis=0, keepdims=True)
            acc_ref[g] = jnp.zeros((VT_ROWS, ROWS), F32)

    def compress_and_select(nb):
        n_iota = lax.broadcasted_iota(jnp.int32, (nb, ROWS), 0)
        j_s = lax.broadcasted_iota(jnp.int32, (nb, TQ), 0)
        j_f = j_s.astype(F32)
        for g in groups:
            gs = slice(g * HEAD_DIM, (g + 1) * HEAD_DIM)
            s_c = _dot(kc[:nb], qa_ref[g, 128:256, :])
            bias_c = (jnp.where(n_iota == n1, d1_ref[g, 0:1, :], 0.0)
                      + jnp.where(n_iota == n1 - 1, d2_ref[g, 0:1, :], 0.0))
            valid_c = n_iota <= n1
            l_c = jnp.where(valid_c, s_c + bias_c, NEG)
            m_c = jnp.max(l_c, axis=0, keepdims=True)
            e_c = jnp.where(valid_c, jnp.exp2(l_c - m_c), 0.0)
            p_c = e_c / jnp.maximum(jnp.sum(e_c, axis=0, keepdims=True), 1e-30)
            p_all = p_c.astype(BF16)
            if nb < n_blocks:
                p_all = jnp.concatenate([p_all, jnp.zeros((n_blocks - nb, ROWS), BF16)], axis=0)
            oc_ref[g] = _dot(vct_ref[gs, :], p_all)

            imp = p_c[:, 0:TQ]
            for hl in range(1, NSA_HPG):
                imp = imp + p_c[:, hl * TQ:(hl + 1) * TQ]
            forced = (j_s == 0) | (j_s == cur) | (j_s == cur - 1)
            candidates = jnp.where(forced, NEG, jnp.where(j_s <= cur, imp, NEG))
            work = candidates
            for _ in range(SEL_TOPN - 3):
                mx = jnp.max(work, axis=0, keepdims=True)
                first = jnp.min(jnp.where(work == mx, j_f, float(nb)), axis=0, keepdims=True)
                work = jnp.where(j_f == first, -jnp.inf, work)
            picked_mask = jnp.where(candidates > NEG / 2, jnp.where(work == -jnp.inf, 0.0, MASKV), MASKV)
            sel_mask = jnp.where(forced, 0.0, picked_mask).astype(BF16)
            qa_ref[g, 0:nb, :] = jnp.concatenate([sel_mask] * NSA_HPG, axis=1)
            if nb < n_blocks:
                qa_ref[g, nb:n_blocks, :] = jnp.full((n_blocks - nb, ROWS), MASKV, BF16)

    tiles_per_span = n_blocks // SELECT_SPANS // blocks_per_tile
    for k in range(SELECT_SPANS):
        @pl.when(qt // tiles_per_span == k)
        def _(k=k):
            compress_and_select((k + 1) * n_blocks // SELECT_SPANS)

    def sel_chains(cs, biases=None):
        keys = [ka_ref[chunk_rows(c), :] for c in cs]
        biases = biases or [None] * len(cs)
        return [([(k, qa_ref[g], bias if bias is None else bias[g], vst_ref[c, g])
                  for k, c, bias in zip(keys, cs, biases)], ms_ref.at[g], accs_ref.at[g]) for g in groups]

    def win_chains(cs, biases):
        keys = [kw_ref[0, chunk_rows(c), :] for c in cs]
        return [([(k, qa_ref[g, 128:256, :], bias[g], vwt_ref[c, g])
                  for k, c, bias in zip(keys, cs, biases)], mw_ref.at[g], accw_ref.at[g]) for g in groups]

    @pl.when(qt >= 2)
    def _():
        w2_mask = jnp.where(q_loc < lax.broadcasted_iota(jnp.int32, (CK, ROWS), 0), 0.0, MASKV)
        ta, tb = [ta_ref[g] for g in groups], [tb_ref[g] for g in groups]
        _attend_from_reference(sel_chains([qt, qt - 1], [tb, ta])
                               + win_chains([qt, qt - 1, qt - 2], [tb, ta, [w2_mask] * NSA_KV_HEADS]), TAIL_DEPTH)

    @pl.when(qt == 1)
    def _():
        ta, tb = [ta_ref[g] for g in groups], [tb_ref[g] for g in groups]
        _attend_from_reference(sel_chains([1, 0], [tb, ta]) + win_chains([1, 0], [tb, ta]), TAIL_DEPTH)

    @pl.when(qt == 0)
    def _():
        tb = [tb_ref[g] for g in groups]
        _attend_from_reference(sel_chains([0], [tb]) + win_chains([0], [tb]), TAIL_DEPTH)

    n_far = jnp.maximum(qt - 1, 0)

    def far_block(i, carry):
        _attend_from_reference(sel_chains([FAR_BLOCK * i + k for k in range(FAR_BLOCK)]), FAR_DEPTH)
        return carry
    lax.fori_loop(0, n_far // FAR_BLOCK, far_block, 0)
    done = (n_far // FAR_BLOCK) * FAR_BLOCK
    size = FAR_BLOCK // 2
    while size:
        @pl.when((n_far & size) != 0)
        def _(done=done, size=size):
            _attend_from_reference(sel_chains([done + k for k in range(size)]), FAR_DEPTH)
        done = done + (n_far & size)
        size //= 2

    for g in range(NSA_KV_HEADS):
        o_s = accs_ref[g, 0:HEAD_DIM, :] / accs_ref[g, HEAD_DIM:HEAD_DIM + 1, :]
        o_w = accw_ref[g, 0:HEAD_DIM, :] / accw_ref[g, HEAD_DIM:HEAD_DIM + 1, :]

        def gate(branch):
            return jnp.concatenate(
                [gates_t[(g * NSA_HPG + hl) * 3 + branch:(g * NSA_HPG + hl) * 3 + branch + 1, :]
                 for hl in range(NSA_HPG)], axis=1)
        mixed = gate(0) * oc_ref[g] + gate(1) * o_s + gate(2) * o_w
        for hl in range(NSA_HPG):
            hd = g * NSA_HPG + hl
            outt_ref[hd * HEAD_DIM:(hd + 1) * HEAD_DIM, :] = mixed[:, hl * TQ:(hl + 1) * TQ]

    out_ref[0] = outt_ref[...].T


def _attention(q, gates, kc, vc, kv, tabs):
    b, s, _ = q.shape
    n_blocks = s // CMP_BLOCK
    nch = s // CK
    ta, tb, d1, d2 = tabs
    tile = lambda w: pl.BlockSpec((1, TQ, w), lambda bi, qi: (bi, qi, 0))
    per_b = lambda r, w, col: pl.BlockSpec((1, r, w), lambda bi, qi: (bi, 0, col), pipeline_mode=pl.Buffered(1))
    return pl.pallas_call(
        _attn_kernel,
        grid=(b, s // TQ),
        in_specs=[tile(NSA_WIDTH), tile(128),
                  per_b(n_blocks, KV_WIDTH, 0), per_b(n_blocks, KV_WIDTH, 0),
                  per_b(s, KV_WIDTH, 0), per_b(s, KV_WIDTH, 1), per_b(s, KV_WIDTH, 2), per_b(s, KV_WIDTH, 3),
                  _const_spec(ta.shape), _const_spec(tb.shape), _const_spec(d1.shape), _const_spec(d2.shape)],
        out_specs=tile(NSA_WIDTH),
        out_shape=jax.ShapeDtypeStruct((b, s, NSA_WIDTH), F32),
        scratch_shapes=[pltpu.VMEM((s, CK), BF16),
                        pltpu.VMEM((nch, NSA_KV_HEADS, VT_ROWS, CK), BF16),
                        pltpu.VMEM((nch, NSA_KV_HEADS, VT_ROWS, CK), BF16),
                        pltpu.VMEM((KV_WIDTH, n_blocks), BF16),
                        pltpu.VMEM((NSA_KV_HEADS, CK, ROWS), BF16),
                        pltpu.VMEM((NSA_KV_HEADS, HEAD_DIM, ROWS), F32),
                        pltpu.VMEM((NSA_KV_HEADS, 1, ROWS), F32), pltpu.VMEM((NSA_KV_HEADS, VT_ROWS, ROWS), F32),
                        pltpu.VMEM((NSA_KV_HEADS, 1, ROWS), F32), pltpu.VMEM((NSA_KV_HEADS, VT_ROWS, ROWS), F32),
                        pltpu.VMEM((NSA_WIDTH, TQ), F32)],
        compiler_params=_params(("parallel", "arbitrary")),
        name="nsa_attention",
    )(q, gates, kc, vc, kv, kv, kv, kv, ta, tb, d1, d2)


def _lru_kernel(xgr_ref, cw_ref, cb_ref, wa_ref, ba_ref, wx_ref, bx_ref, lam_ref,
                out_ref, xbuf_ref, h_ref):
    t = pl.program_id(1)
    ts = LRU_TILE
    w = out_ref.shape[2]

    @pl.when(t == 0)
    def _():
        xbuf_ref[0:8, :] = jnp.zeros((8, w), F32)
        h_ref[...] = jnp.zeros_like(h_ref)

    xg = xgr_ref[0, :, :w]
    xbuf_ref[8:, :] = xgr_ref[0, :, w:]
    xc = cb_ref[...] + xbuf_ref[8:, :] * cw_ref[CONV_WIDTH - 1:CONV_WIDTH, :]
    for k in range(1, CONV_WIDTH):
        xc = xc + xbuf_ref[8 - k:8 - k + ts, :] * cw_ref[CONV_WIDTH - 1 - k:CONV_WIDTH - k, :]
    xbuf_ref[0:8, :] = xbuf_ref[ts:ts + 8, :]

    xcb = xc.astype(BF16)
    r = jax.nn.sigmoid(_dot(xcb, wa_ref[...]) + ba_ref[...])
    i = jax.nn.sigmoid(_dot(xcb, wx_ref[...]) + bx_ref[...])
    nl = -lam_ref[...]
    softplus = jnp.maximum(nl, 0.0) + jnp.log(1.0 + jnp.exp(-jnp.abs(nl)))
    log_a = (-LRU_C * softplus) * r
    a = jnp.exp(log_a)
    bv = jnp.sqrt(1.0 - a * a) * (i * xc)

    row8 = lax.broadcasted_iota(jnp.int32, (8, w), 0)
    y = jax.nn.gelu(xg)
    hc = h_ref[0:1, :]
    for grp in range(ts // 8):
        rs = slice(8 * grp, 8 * grp + 8)
        ag, bg = a[rs], bv[rs]
        sh = 1
        while sh < 8:
            a_prev = jnp.where(row8 >= sh, pltpu.roll(ag, sh, 0), 1.0)
            b_prev = jnp.where(row8 >= sh, pltpu.roll(bg, sh, 0), 0.0)
            bg = ag * b_prev + bg
            ag = ag * a_prev
            sh *= 2
        hg = bg + ag * hc
        out_ref[0, rs, :] = hg * y[rs]
        hc = hg[7:8]
    h_ref[0:1, :] = hc


def _lru(xgr, cw, cb, wa, ba, wx, bx, lam):
    b, s, w2 = xgr.shape
    w = w2 // 2
    ts = LRU_TILE
    return pl.pallas_call(
        _lru_kernel,
        grid=(b, s // ts),
        in_specs=[pl.BlockSpec((1, ts, w2), lambda bi, ti: (bi, ti, 0)),
                  _const_spec(cw.shape), _const_spec(cb.shape), _const_spec(wa.shape), _const_spec(ba.shape),
                  _const_spec(wx.shape), _const_spec(bx.shape), _const_spec(lam.shape)],
        out_specs=pl.BlockSpec((1, ts, w), lambda bi, ti: (bi, ti, 0)),
        out_shape=jax.ShapeDtypeStruct((b, s, w), F32),
        scratch_shapes=[pltpu.VMEM((ts + 8, w), F32), pltpu.VMEM((8, w), F32)],
        compiler_params=_params(("parallel", "arbitrary")),
        name="rg_lru",
    )(xgr, cw, cb, wa, ba, wx, bx, lam)


def _post_kernel(x_ref, att_ref, lru_ref, kt_ref, v_ref,
                 ga_ref, gl_ref, wa_ref, wl_ref, gmix_ref,
                 gxpre_ref, wq_ref, wo_ref, gxpost_ref,
                 gmpre_ref, w1_ref, w2_ref, gmpost_ref, out_ref):
    tm, d = x_ref.shape[1], x_ref.shape[2]
    parts = [slice(k * tm // ROW_SPLIT, (k + 1) * tm // ROW_SPLIT) for k in range(ROW_SPLIT)]
    xs = [x_ref[0, r, :] for r in parts]
    a = [_rms(att_ref[0, r, :], ga_ref[...]).astype(BF16) for r in parts]
    l = [_rms(lru_ref[0, r, :], gl_ref[...]).astype(BF16) for r in parts]
    mixed = [_dot(ak, wa_ref[...]) + _dot(lk, wl_ref[...]) for ak, lk in zip(a, l)]
    xs = [x + _rms(mk, gmix_ref[...]) for x, mk in zip(xs, mixed)]

    dh = d // X_HEADS
    hq = [_rms(x, gxpre_ref[...]).astype(BF16) for x in xs]
    cq = [(_dot(h, wq_ref[...]) * (dh ** -0.5 * LOG2E)).astype(BF16) for h in hq]
    outs = [[] for _ in parts]
    for hh in range(X_HEADS):
        hs = slice(hh * dh, (hh + 1) * dh)
        scores = [_dot(c[:, hs], kt_ref[0, hs, :]) for c in cq]
        for out, s in zip(outs, scores):
            e = jnp.exp2(s - jnp.max(s, axis=1, keepdims=True))
            p = e / jnp.sum(e, axis=1, keepdims=True)
            out.append(_dot(p.astype(BF16), v_ref[0, :, hs]).astype(BF16))
    co = [_dot(jnp.concatenate(out, axis=1), wo_ref[...]) for out in outs]
    xs = [x + _rms(c, gxpost_ref[...]) for x, c in zip(xs, co)]

    hm = [_rms(x, gmpre_ref[...]).astype(BF16) for x in xs]
    accs = [jnp.zeros(x.shape, F32) for x in xs]
    for c in range(w1_ref.shape[1] // FF_CHUNK):
        cs = slice(c * FF_CHUNK, (c + 1) * FF_CHUNK)
        us = [jnp.maximum(_dot(h, w1_ref[:, cs]), 0.0) for h in hm]
        accs = [acc + _dot((u * u).astype(BF16), w2_ref[cs, :]) for acc, u in zip(accs, us)]
    for r, x, acc in zip(parts, xs, accs):
        out_ref[0, r, :] = x + _rms(acc, gmpost_ref[...])


def _post(x, att, lru, kt, v, gains_and_weights):
    b, s, d = x.shape
    m = v.shape[1]
    tm = ROW_TILE
    row = lambda w: pl.BlockSpec((1, tm, w), lambda bi, i: (bi, i, 0))
    resident = lambda a: pl.BlockSpec(a.shape, lambda *_: (0,) * a.ndim, pipeline_mode=pl.Buffered(1))
    return pl.pallas_call(
        _post_kernel,
        grid=(b, s // tm),
        in_specs=[row(d), row(att.shape[2]), row(lru.shape[2]),
                  pl.BlockSpec((1, d, m), lambda bi, i: (bi, 0, 0)),
                  pl.BlockSpec((1, m, d), lambda bi, i: (bi, 0, 0))]
                 + [resident(a) for a in gains_and_weights],
        out_specs=row(d),
        out_shape=jax.ShapeDtypeStruct((b, s, d), F32),
        compiler_params=_params(("parallel", "parallel")),
        name="post_mixer",
    )(x, att, lru, kt, v, *gains_and_weights)


def _memkv_kernel(mem_ref, g_ref, wk_ref, wv_ref, kt_ref, v_ref):
    mn = _rms(mem_ref[0], g_ref[...]).astype(BF16)
    kt_ref[0] = _dot(mn, wk_ref[...]).T.astype(BF16)
    v_ref[0] = _dot(mn, wv_ref[...]).astype(BF16)


def _memkv(mem, g, wk, wv):
    b, m, d = mem.shape
    return pl.pallas_call(
        _memkv_kernel,
        grid=(b,),
        in_specs=[pl.BlockSpec((1, m, d), lambda i: (i, 0, 0)), _const_spec(g.shape),
                  _const_spec(wk.shape), _const_spec(wv.shape)],
        out_specs=[pl.BlockSpec((1, d, m), lambda i: (i, 0, 0)), pl.BlockSpec((1, m, d), lambda i: (i, 0, 0))],
        out_shape=[jax.ShapeDtypeStruct((b, d, m), BF16), jax.ShapeDtypeStruct((b, m, d), BF16)],
        compiler_params=_params(("parallel",)),
        name="mem_kv",
    )(mem, g, wk, wv)


def _block_structured(w):
    eye = jnp.eye(NSA_KV_HEADS, dtype=w.dtype)
    big = w[:, None, :, None, :] * eye[None, :, None, :, None]
    return big.reshape(CMP_BLOCK * KV_WIDTH, KV_WIDTH).astype(BF16)


def _block_diag(w):
    n, d, e = w.shape
    eye = jnp.eye(n, dtype=w.dtype)
    return (w[:, :, None, :] * eye[:, None, :, None]).reshape(n * d, n * e).astype(BF16)


def kernel(x, mem, rel_bias, ln_mix_pre, ln_mix_post, w_in, cmp_pe_k, cmp_pe_v, cmp_w_k, cmp_w_v,
           conv_w, conv_b, lru_wa, lru_ba, lru_wx, lru_bx, lru_lambda, gn_attn, gn_lru, w_out,
           ln_x_pre, ln_x_post, ln_mem, xq, xkv, xo, ln_mlp_pre, ln_mlp_post, mlp_w1, mlp_w2):
    b, s, d = x.shape
    depth = w_in.shape[0]
    n = b * s
    lru_w = conv_w.shape[2]
    assert s % TQ == 0 and s % LRU_TILE == 0 and n % (2 * ROW_TILE) == 0 and TQ == CK
    assert s // CMP_BLOCK == 128, "selection mask layout assumes 128 compression blocks"
    row1 = lambda v: v.reshape(1, -1)

    tabs = _bias_tables(rel_bias)
    x2 = x.reshape(n, d)
    o_q = NSA_WIDTH
    o_g = o_q + 6 * KV_WIDTH
    o_x = o_g + 3 * NSA_HEADS
    for l in range(depth):
        wl = w_in[l]
        wq = wl[:, :o_q].astype(BF16)
        wc = wl[:, o_q:o_q + 2 * KV_WIDTH].astype(BF16)
        wkv = wl[:, o_q + 2 * KV_WIDTH:o_g].astype(BF16)
        wg = jnp.pad(wl[:, o_g:o_x], ((0, 0), (0, 128 - 3 * NSA_HEADS))).astype(BF16)
        wx = wl[:, o_x:].astype(BF16)
        q, kcr, vcr, kv, gt, xgr = _inproj(x2, row1(ln_mix_pre[l]), wq, wc, wkv, wg, wx)

        nblk = n // CMP_BLOCK
        pek = jnp.tile(cmp_pe_k[l], (1, NSA_KV_HEADS)).reshape(1, -1)
        pev = jnp.tile(cmp_pe_v[l], (1, NSA_KV_HEADS)).reshape(1, -1)
        kc, vc = _compress(kcr.reshape(nblk, CMP_BLOCK * KV_WIDTH), vcr.reshape(nblk, CMP_BLOCK * KV_WIDTH),
                           pek, pev, _block_structured(cmp_w_k[l]), _block_structured(cmp_w_v[l]))

        att = _attention(q.reshape(b, s, NSA_WIDTH), gt.reshape(b, s, 128),
                         kc.reshape(b, s // CMP_BLOCK, KV_WIDTH), vc.reshape(b, s // CMP_BLOCK, KV_WIDTH),
                         kv.reshape(b, s, 4 * KV_WIDTH), tabs)
        lru = _lru(xgr.reshape(b, s, 2 * lru_w), conv_w[l], row1(conv_b[l]),
                   _block_diag(lru_wa[l]), row1(lru_ba[l]), _block_diag(lru_wx[l]), row1(lru_bx[l]),
                   row1(lru_lambda[l]))

        wo = w_out[l].astype(BF16)
        wkv_x = xkv[l].astype(BF16)
        kt, v = _memkv(mem, row1(ln_mem[l]), wkv_x[:, :d], wkv_x[:, d:])
        x2 = _post(x2.reshape(b, s, d), att, lru, kt, v,
                   (row1(gn_attn[l]), row1(gn_lru[l]), wo[:NSA_WIDTH], wo[NSA_WIDTH:], row1(ln_mix_post[l]),
                    row1(ln_x_pre[l]), xq[l].astype(BF16), xo[l].astype(BF16), row1(ln_x_post[l]),
                    row1(ln_mlp_pre[l]), mlp_w1[l].astype(BF16), mlp_w2[l].astype(BF16),
                    row1(ln_mlp_post[l]))).reshape(n, d)
    return x2.reshape(b, s, d)
```
